```python
import jax, jax.numpy as jnp
from jax import lax
import numpy as np

D_MODEL = 2048
BATCH = 4
SEQ = 4096
DEPTH = 1

EPS = 1e-6
SSM_D_INNER = 2048
SSM_HEAD_DIM = 64
SSM_HEADS = SSM_D_INNER // SSM_HEAD_DIM
SSM_GROUPS = 8
SSM_STATE = 128
SSM_CONV_K = 5
SSM_CHUNK = 128
SSM_XBC = SSM_D_INNER + 2 * SSM_GROUPS * SSM_STATE
ATTN_HEAD_DIM = 128
ATTN_Q_HEADS = 16
ATTN_KV_HEADS = 4
ATTN_Q_BLOCK = 128
ROPE_THETA = 10000.0
GRID_W = 64
D_FF = 4 * D_MODEL
IN_SIZES = (SSM_D_INNER, SSM_XBC, SSM_HEADS, SSM_HEADS,
            ATTN_Q_HEADS * ATTN_HEAD_DIM, ATTN_KV_HEADS * ATTN_HEAD_DIM, ATTN_KV_HEADS * ATTN_HEAD_DIM,
            D_MODEL, D_MODEL)
N_IN = sum(IN_SIZES)

kernel_name = "hybrid_ssd_axial_gqa_gated_encoder"


def rms_norm(x, g):
    xf = x.astype(jnp.float32)
    y = xf * lax.rsqrt(jnp.mean(xf * xf, axis=-1, keepdims=True) + EPS)
    return (y * g.astype(jnp.float32)).astype(x.dtype)


def split_points(sizes):
    pts, acc = [], 0
    for s in sizes[:-1]:
        acc += s
        pts.append(acc)
    return pts


def centred_depthwise_conv(u, w, b):
    c = u.shape[-1]
    k = w.shape[0]
    out = lax.conv_general_dilated(
        u, w.reshape(k, 1, c).astype(u.dtype), window_strides=(1,),
        padding=[(k // 2, k // 2)], dimension_numbers=("NWC", "WIO", "NWC"),
        feature_group_count=c)
    return out + b.astype(u.dtype)


def ssd_scan(xh, a, bg, cg, chunk):
    b, s, h, p = xh.shape
    g, n = bg.shape[-2:]
    r = h // g
    c = s // chunk
    x = xh.reshape(b, c, chunk, g, r, p)
    a = a.reshape(b, c, chunk, g, r)
    bm = bg.reshape(b, c, chunk, g, n)
    cm = cg.reshape(b, c, chunk, g, n)
    a_cum = jnp.cumsum(a, axis=2)
    seg = a_cum[:, :, :, None] - a_cum[:, :, None, :]
    mask = jnp.tril(jnp.ones((chunk, chunk), dtype=bool))[:, :, None, None]
    decay = jnp.exp(jnp.where(mask, seg, -jnp.inf))
    scores = jnp.einsum('bclgn,bcsgn->bclsg', cm, bm)
    y_diag = jnp.einsum('bclsg,bclsgr,bcsgrp->bclgrp', scores, decay, x)
    decay_to_end = jnp.exp(a_cum[:, :, -1:] - a_cum)
    chunk_states = jnp.einsum('bclgn,bclgr,bclgrp->bcgrpn', bm, decay_to_end, x)
    chunk_decay = jnp.exp(a_cum[:, :, -1])

    def step(state, inp):
        st, dec = inp
        return state * dec[..., None, None] + st, state

    init = jnp.zeros((b, g, r, p, n), chunk_states.dtype)
    _, states_in = lax.scan(step, init, (jnp.moveaxis(chunk_states, 1, 0), jnp.moveaxis(chunk_decay, 1, 0)))
    states_in = jnp.moveaxis(states_in, 0, 1)
    y_off = jnp.einsum('bclgn,bcgrpn,bclgr->bclgrp', cm, states_in, jnp.exp(a_cum))
    return (y_diag + y_off).reshape(b, s, h, p)


def mamba2_bidir(z, xbc, dt_f, dt_b, conv_w, conv_b, dt_bias_f, dt_bias_b, a_log_f, a_log_b, d_skip, norm_g):
    b, s, _ = z.shape
    xbc = jax.nn.silu(centred_depthwise_conv(xbc, conv_w, conv_b))
    xs, bm, cm = jnp.split(xbc, [SSM_D_INNER, SSM_D_INNER + SSM_GROUPS * SSM_STATE], axis=-1)
    xs = xs.reshape(b, s, SSM_HEADS, SSM_HEAD_DIM)
    bm = bm.reshape(b, s, SSM_GROUPS, SSM_STATE)
    cm = cm.reshape(b, s, SSM_GROUPS, SSM_STATE)

    def direction(xs_d, bm_d, cm_d, dt_raw, dt_bias, a_log):
        dt = jax.nn.softplus(dt_raw.astype(jnp.float32) + dt_bias.astype(jnp.float32))
        a = -jnp.exp(a_log.astype(jnp.float32)) * dt
        return ssd_scan(xs_d * dt[..., None], a, bm_d, cm_d, SSM_CHUNK)

    flip = lambda t: jnp.flip(t, axis=1)
    y_f = direction(xs, bm, cm, dt_f, dt_bias_f, a_log_f)
    y_b = flip(direction(flip(xs), flip(bm), flip(cm), flip(dt_b), dt_bias_b, a_log_b))
    y = y_f + y_b + d_skip.astype(jnp.float32)[:, None] * xs
    y = y.reshape(b, s, SSM_D_INNER) * jax.nn.silu(z.astype(jnp.float32))
    y = rms_norm(y.reshape(b, s, SSM_GROUPS, SSM_D_INNER // SSM_GROUPS),
                 norm_g.reshape(SSM_GROUPS, SSM_D_INNER // SSM_GROUPS))
    return y.reshape(b, s, SSM_D_INNER).astype(z.dtype)


def axial_rope_tables(seq):
    rows = seq // GRID_W
    row = jnp.repeat(jnp.arange(rows, dtype=jnp.float32), GRID_W)
    col = jnp.tile(jnp.arange(GRID_W, dtype=jnp.float32), rows)
    half = ATTN_HEAD_DIM // 2
    inv_freq = 1.0 / (ROPE_THETA ** (jnp.arange(0, half, 2, dtype=jnp.float32) / half))
    ang = jnp.concatenate([row[:, None] * inv_freq, col[:, None] * inv_freq], axis=-1)
    return jnp.cos(ang), jnp.sin(ang)


def apply_rope(x, cos, sin):
    b, s, h, d = x.shape
    xf = x.astype(jnp.float32).reshape(b, s, h, d // 2, 2)
    x0, x1 = xf[..., 0], xf[..., 1]
    c = cos[None, :, None, :]
    sn = sin[None, :, None, :]
    out = jnp.stack([x0 * c - x1 * sn, x0 * sn + x1 * c], axis=-1)
    return out.reshape(b, s, h, d).astype(x.dtype)


def gqa_bidir_blocks(q, k, v):
    b, s, hq, d = q.shape
    hkv = k.shape[2]
    r = hq // hkv
    nblk = s // ATTN_Q_BLOCK
    scale = d ** -0.5
    qb = (q * scale).reshape(b, nblk, ATTN_Q_BLOCK, hkv, r, d).transpose(1, 0, 2, 3, 4, 5)

    def one_block(qblk):
        sc = jnp.einsum('blkrd,bskd->bkrls', qblk, k).astype(jnp.float32)
        prob = jax.nn.softmax(sc, axis=-1).astype(v.dtype)
        return jnp.einsum('bkrls,bskd->blkrd', prob, v)

    out = lax.map(one_block, qb)
    return out.transpose(1, 0, 2, 3, 4, 5).reshape(b, s, hq * d)


def setup_inputs(seed: int = 0) -> dict:
    key = jax.random.key(seed)
    ks = jax.random.split(key, 24)
    f32 = jnp.float32
    nrm = lambda k, shape, scale: jax.random.normal(k, shape, f32) * scale
    gain = lambda k, shape: 1.0 + 0.02 * jax.random.normal(k, shape, f32)
    L = DEPTH
    dt_lo, dt_hi = 1e-3, 1e-1

    def dt_bias(k):
        u = jax.random.uniform(k, (L, SSM_HEADS), f32)
        dt = jnp.exp(u * (np.log(dt_hi) - np.log(dt_lo)) + np.log(dt_lo))
        return dt + jnp.log(-jnp.expm1(-dt))

    def a_log(k):
        return jnp.log(jax.random.uniform(k, (L, SSM_HEADS), f32, 1.0, 16.0))

    return {
        "x": jax.random.normal(ks[0], (BATCH, SEQ, D_MODEL), f32),
        "g_mix": gain(ks[1], (L, D_MODEL)),
        "w_in": nrm(ks[2], (L, D_MODEL, N_IN), D_MODEL ** -0.5),
        "conv_w": nrm(ks[3], (L, SSM_CONV_K, SSM_XBC), SSM_CONV_K ** -0.5),
        "conv_b": nrm(ks[4], (L, SSM_XBC), 0.02),
        "dt_bias_f": dt_bias(ks[5]),
        "dt_bias_b": dt_bias(ks[6]),
        "a_log_f": a_log(ks[7]),
        "a_log_b": a_log(ks[8]),
        "d_skip": gain(ks[9], (L, SSM_HEADS)),
        "ssm_norm_g": gain(ks[10], (L, SSM_D_INNER)),
        "q_norm_g": gain(ks[11], (L, ATTN_HEAD_DIM)),
        "k_norm_g": gain(ks[12], (L, ATTN_HEAD_DIM)),
        "w_ssm_up": nrm(ks[13], (L, SSM_D_INNER, D_MODEL), SSM_D_INNER ** -0.5),
        "w_attn_up": nrm(ks[14], (L, ATTN_Q_HEADS * ATTN_HEAD_DIM, D_MODEL), (ATTN_Q_HEADS * ATTN_HEAD_DIM) ** -0.5),
        "w_out": nrm(ks[15], (L, D_MODEL, D_MODEL), D_MODEL ** -0.5),
        "g_mlp": gain(ks[16], (L, D_MODEL)),
        "w_mlp_in": nrm(ks[17], (L, D_MODEL, D_FF), D_MODEL ** -0.5),
        "w_mlp_out": nrm(ks[18], (L, D_FF, D_MODEL), D_FF ** -0.5),
        "g_final": gain(ks[19], (D_MODEL,)),
    }


def reference(x, g_mix, w_in, conv_w, conv_b, dt_bias_f, dt_bias_b, a_log_f, a_log_b, d_skip,
              ssm_norm_g, q_norm_g, k_norm_g, w_ssm_up, w_attn_up, w_out, g_mlp, w_mlp_in,
              w_mlp_out, g_final):
    b, s, _ = x.shape
    cos, sin = axial_rope_tables(s)
    for layer in range(DEPTH):
        h = rms_norm(x, g_mix[layer])
        proj = h @ w_in[layer]
        z, xbc, dt_f, dt_b, q, k, v, gate_a, gate_b = jnp.split(proj, split_points(IN_SIZES), axis=-1)

        y_ssm = mamba2_bidir(z, xbc, dt_f, dt_b, conv_w[layer], conv_b[layer],
                             dt_bias_f[layer], dt_bias_b[layer], a_log_f[layer], a_log_b[layer],
                             d_skip[layer], ssm_norm_g[layer])

        q = rms_norm(q.reshape(b, s, ATTN_Q_HEADS, ATTN_HEAD_DIM), q_norm_g[layer])
        k = rms_norm(k.reshape(b, s, ATTN_KV_HEADS, ATTN_HEAD_DIM), k_norm_g[layer])
        v = v.reshape(b, s, ATTN_KV_HEADS, ATTN_HEAD_DIM)
        q = apply_rope(q, cos, sin)
        k = apply_rope(k, cos, sin)
        y_attn = gqa_bidir_blocks(q, k, v)

        branch_a = y_ssm @ w_ssm_up[layer]
        branch_b = y_attn @ w_attn_up[layer]
        merged = jax.nn.sigmoid(gate_a) * branch_a + jax.nn.sigmoid(gate_b) * branch_b
        x = x + merged @ w_out[layer]
        h2 = rms_norm(x, g_mlp[layer])
        x = x + jnp.square(jax.nn.relu(h2 @ w_mlp_in[layer])) @ w_mlp_out[layer]
    return rms_norm(x, g_final)
```

```python
import functools

import jax
import jax.numpy as jnp
import numpy as np
from jax import lax
from jax.experimental import pallas as pl
from jax.experimental.pallas import tpu as pltpu

F32 = jnp.float32
BF16 = jnp.bfloat16

EPS = 1e-6
D_MODEL = 2048
SSM_D_INNER = 2048
SSM_HEAD_DIM = 64
SSM_HEADS = SSM_D_INNER // SSM_HEAD_DIM
SSM_GROUPS = 8
SSM_HPG = SSM_HEADS // SSM_GROUPS
SSM_GW = SSM_D_INNER // SSM_GROUPS
SSM_STATE = 128
SSM_CONV_K = 5
SSM_CHUNK = 128
SSM_XBC = SSM_D_INNER + 2 * SSM_GROUPS * SSM_STATE
ATTN_HEAD_DIM = 128
ATTN_Q_HEADS = 16
ATTN_KV_HEADS = 4
ATTN_REP = ATTN_Q_HEADS // ATTN_KV_HEADS
ROPE_THETA = 10000.0
GRID_W = 64
D_FF = 4 * D_MODEL

COL_Z = 0
COL_XS = COL_Z + SSM_D_INNER
COL_B = COL_XS + SSM_D_INNER
COL_C = COL_B + SSM_GROUPS * SSM_STATE
COL_Q = COL_C + SSM_GROUPS * SSM_STATE
COL_K = COL_Q + ATTN_Q_HEADS * ATTN_HEAD_DIM
COL_V = COL_K + ATTN_KV_HEADS * ATTN_HEAD_DIM
COL_GA = COL_V + ATTN_KV_HEADS * ATTN_HEAD_DIM
COL_GB = COL_GA + D_MODEL
N_MAIN = COL_GB + D_MODEL
DT_PAD = 128

V7X_VMEM_LIMIT = 56 * 1024 * 1024
CONV_HALO = 16
CONV_WIN = SSM_CHUNK + 2 * CONV_HALO


def _params(*sem):
    return pltpu.CompilerParams(dimension_semantics=sem, vmem_limit_bytes=V7X_VMEM_LIMIT)


def _sigmoid(x):
    return 1.0 / (1.0 + jnp.exp(-x))


def _softplus(x):
    return jnp.maximum(x, 0.0) + jnp.log(1.0 + jnp.exp(-jnp.abs(x)))


def _proj_kernel(x_ref, g_ref, w_ref, wdt_ref, o_ref, dt_ref, h_scr, *, rows):
    @pl.when(pl.program_id(1) == 0)
    def _():
        def body(r, _):
            sl = pl.ds(pl.multiple_of(r * rows, rows), rows)
            x = x_ref[sl, :]
            ms = jnp.mean(x * x, axis=-1, keepdims=True)
            hb = (x * lax.rsqrt(ms + EPS) * g_ref[...]).astype(BF16)
            h_scr[sl, :] = hb
            dt_ref[sl, :] = jnp.dot(hb, wdt_ref[...], preferred_element_type=F32)
            return 0

        lax.fori_loop(0, x_ref.shape[0] // rows, body, 0)

    o_ref[...] = jnp.dot(h_scr[...], w_ref[...], preferred_element_type=F32).astype(o_ref.dtype)


def _proj(x2d, g, w_main, w_dt, *, tm, tn):
    m, d = x2d.shape
    n = w_main.shape[1]
    return pl.pallas_call(
        functools.partial(_proj_kernel, rows=min(tm, 256)),
        grid=(m // tm, n // tn),
        in_specs=[
            pl.BlockSpec((tm, d), lambda i, j: (i, 0)),
            pl.BlockSpec((1, d), lambda i, j: (0, 0)),
            pl.BlockSpec((d, tn), lambda i, j: (0, j)),
            pl.BlockSpec((d, DT_PAD), lambda i, j: (0, 0)),
        ],
        out_specs=[
            pl.BlockSpec((tm, tn), lambda i, j: (i, j)),
            pl.BlockSpec((tm, DT_PAD), lambda i, j: (i, 0)),
        ],
        out_shape=[
            jax.ShapeDtypeStruct((m, n), BF16),
            jax.ShapeDtypeStruct((m, DT_PAD), F32),
        ],
        scratch_shapes=[pltpu.VMEM((tm, d), BF16)],
        compiler_params=_params("parallel", "arbitrary"),
        name="proj",
    )(x2d, g, w_main, w_dt)


def _conv_silu_chunk(u_ref, w_ref, b_ref, c, seq):
    L = SSM_CHUNK
    start = jnp.clip(c * L - CONV_HALO, 0, seq - CONV_WIN)
    start = pl.multiple_of(start, CONV_HALO)
    off = c * L - start
    u = u_ref[pl.ds(start, CONV_WIN), :]
    t_idx = lax.broadcasted_iota(jnp.int32, (L, CONV_WIN), 0)
    j_idx = lax.broadcasted_iota(jnp.int32, (L, CONV_WIN), 1)
    base = j_idx - t_idx - off
    shifts = [jnp.where(base == k - SSM_CONV_K // 2, 1.0, 0.0).astype(BF16) for k in range(SSM_CONV_K)]
    r = jnp.dot(jnp.concatenate(shifts, axis=0), u, preferred_element_type=F32)
    acc = r[0:L] * w_ref[0:1, :] + b_ref[...]
    for k in range(1, SSM_CONV_K):
        acc = acc + r[k * L:(k + 1) * L] * w_ref[k:k + 1, :]
    return acc * _sigmoid(acc)


def _split3(a):
    a1 = a.astype(BF16)
    r1 = a - a1.astype(F32)
    a2 = r1.astype(BF16)
    a3 = (r1 - a2.astype(F32)).astype(BF16)
    return a1, a2, a3


def _ssd_kernel(z_ref, xs_ref, b_ref, c_ref, dt_ref, cwx_ref, cwb_ref, cwc_ref, cbx_ref, cbb_ref,
                cbc_ref, dtb_ref, alog_ref, dsk_ref, ng_ref, y_ref,
                x_scr, bt_scr, c_scr, rows_scr, cols_scr, yacc_scr, h_scr):
    L = SSM_CHUNK
    H = SSM_HPG
    seq = xs_ref.shape[0]
    nchunks = seq // L

    s_i = lax.broadcasted_iota(jnp.int32, (L, L), 0)
    t_i = lax.broadcasted_iota(jnp.int32, (L, L), 1)
    tri_pre = jnp.where(s_i <= t_i, 1.0, 0.0).astype(BF16)
    tri_suf = jnp.where(s_i >= t_i, 1.0, 0.0).astype(BF16)
    neg_a = -jnp.exp(alog_ref[...])

    def prep(c, _):
        rows_sl = pl.ds(pl.multiple_of(c * L, L), L)
        x_scr[rows_sl, :] = _conv_silu_chunk(xs_ref, cwx_ref, cbx_ref, c, seq)
        bm = _conv_silu_chunk(b_ref, cwb_ref, cbb_ref, c, seq)
        bt_scr[c] = bm.T.astype(BF16)
        c_scr[rows_sl, :] = _conv_silu_chunk(c_ref, cwc_ref, cbc_ref, c, seq).astype(BF16)

        dt = _softplus(dt_ref[c] + dtb_ref[...])
        a = neg_a * dt
        a3 = jnp.concatenate(_split3(a), axis=0)
        pre = jnp.dot(a3, tri_pre, preferred_element_type=F32)
        suf = jnp.dot(a3, tri_suf, preferred_element_type=F32)
        pre = pre[0:2 * H] + pre[2 * H:4 * H] + pre[4 * H:6 * H]
        suf = suf[0:2 * H] + suf[2 * H:4 * H] + suf[4 * H:6 * H]
        is_fwd = lax.broadcasted_iota(jnp.int32, (2 * H, L), 0) < H
        rows = jnp.concatenate([jnp.where(is_fwd, pre, suf), dt], axis=0)
        rows_scr[c] = rows
        padded = jnp.concatenate([rows, jnp.zeros((L - 4 * H, L), F32)], axis=0)
        cols_scr[rows_sl, :] = padded.T
        return 0

    lax.fori_loop(0, nchunks, prep, 0)

    lane128 = lax.broadcasted_iota(jnp.int32, (L, L), 1)
    row_h = lax.broadcasted_iota(jnp.int32, (H * L, SSM_GW), 0) // L
    lane_h = lax.broadcasted_iota(jnp.int32, (H * L, SSM_GW), 1) // SSM_HEAD_DIM
    t_w = lax.broadcasted_iota(jnp.int32, (L, H * L), 0)
    s_w = lax.broadcasted_iota(jnp.int32, (L, H * L), 1) % L

    def col_bcast(cols, lane):
        return jnp.broadcast_to(cols[:, lane:lane + 1], (L, L))

    def expand(bc):
        lo = jnp.where(lane128 < SSM_HEAD_DIM, bc[0], bc[1])
        hi = jnp.where(lane128 < SSM_HEAD_DIM, bc[2], bc[3])
        return jnp.concatenate([lo, hi], axis=1)

    def chunk(c, d):
        rows_sl = pl.ds(pl.multiple_of(c * L, L), L)
        rows = rows_scr[c]
        cols = cols_scr[rows_sl, :]
        xc = x_scr[rows_sl, :]
        cm = c_scr[rows_sl, :]
        bt = bt_scr[c]
        scores = jnp.dot(cm, bt, preferred_element_type=F32)
        cum_bc = [col_bcast(cols, d * H + h) for h in range(H)]
        dt_bc = [col_bcast(cols, 2 * H + d * H + h) for h in range(H)]
        seg = (jnp.concatenate(cum_bc, axis=1)
               - jnp.concatenate([rows[d * H + h:d * H + h + 1, :] for h in range(H)], axis=1))
        keep = (s_w <= t_w) if d == 0 else (s_w >= t_w)
        decay = jnp.exp(jnp.where(keep, seg, -jnp.inf))
        mix = (jnp.concatenate([scores] * H, axis=1) * decay).astype(BF16)
        cum_e = expand(cum_bc)
        xdt = xc * expand(dt_bc)
        xdt_b = xdt.astype(BF16)
        x_bd = jnp.where(row_h == lane_h, jnp.concatenate([xdt_b] * H, axis=0), jnp.zeros((), BF16))
        y = jnp.dot(mix, x_bd, preferred_element_type=F32)
        state = h_scr[...]
        y = y + jnp.dot(cm, state.astype(BF16), preferred_element_type=F32) * jnp.exp(cum_e)
        tot = cum_e[L - 1:L, :] if d == 0 else cum_e[0:1, :]
        xe = (xdt * jnp.exp(tot - cum_e)).astype(BF16)
        h_scr[...] = jnp.exp(tot) * state + jnp.dot(bt, xe, preferred_element_type=F32)
        return y

    h_scr[...] = jnp.zeros_like(h_scr)

    def fwd(c, _):
        rows_sl = pl.ds(pl.multiple_of(c * L, L), L)
        yacc_scr[rows_sl, :] = chunk(c, 0) + dsk_ref[...] * x_scr[rows_sl, :]
        return 0

    lax.fori_loop(0, nchunks, fwd, 0)

    h_scr[...] = jnp.zeros_like(h_scr)

    def bwd(i, _):
        c = nchunks - 1 - i
        rows_sl = pl.ds(pl.multiple_of(c * L, L), L)
        y = yacc_scr[rows_sl, :] + chunk(c, 1)
        z = z_ref[rows_sl, :].astype(F32)
        y = y * (z * _sigmoid(z))
        ms = jnp.mean(y * y, axis=-1, keepdims=True)
        y_ref[rows_sl, :] = (y * lax.rsqrt(ms + EPS) * ng_ref[...]).astype(y_ref.dtype)
        return 0

    lax.fori_loop(0, nchunks, bwd, 0)


def _ssd(proj3, dt5, conv_w, conv_b, dtb, alog, dsk, ng):
    bsz, seq, _ = proj3.shape
    L, G, GW, N = SSM_CHUNK, SSM_GROUPS, SSM_GW, SSM_STATE
    nchunks = seq // L
    k = SSM_CONV_K
    col = lambda base, w: (lambda b, g: (b, 0, base // w + g))
    pcol = lambda base, w: (lambda b, g: (0, base // w + g))
    return pl.pallas_call(
        _ssd_kernel,
        grid=(bsz, G),
        in_specs=[
            pl.BlockSpec((None, seq, GW), col(COL_Z, GW)),
            pl.BlockSpec((None, seq, GW), col(COL_XS, GW)),
            pl.BlockSpec((None, seq, N), col(COL_B, N)),
            pl.BlockSpec((None, seq, N), col(COL_C, N)),
            pl.BlockSpec((None, None, nchunks, 2 * SSM_HPG, L), lambda b, g: (b, g, 0, 0, 0)),
            pl.BlockSpec((k, GW), pcol(0, GW)),
            pl.BlockSpec((k, N), pcol(SSM_D_INNER, N)),
            pl.BlockSpec((k, N), pcol(SSM_D_INNER + G * N, N)),
            pl.BlockSpec((1, GW), pcol(0, GW)),
            pl.BlockSpec((1, N), pcol(SSM_D_INNER, N)),
            pl.BlockSpec((1, N), pcol(SSM_D_INNER + G * N, N)),
            pl.BlockSpec((None, 2 * SSM_HPG, 1), lambda b, g: (g, 0, 0)),
            pl.BlockSpec((None, 2 * SSM_HPG, 1), lambda b, g: (g, 0, 0)),
            pl.BlockSpec((1, GW), pcol(0, GW)),
            pl.BlockSpec((1, GW), pcol(0, GW)),
        ],
        out_specs=pl.BlockSpec((None, seq, GW), lambda b, g: (b, 0, g)),
        out_shape=jax.ShapeDtypeStruct((bsz, seq, SSM_D_INNER), BF16),
        scratch_shapes=[
            pltpu.VMEM((seq, GW), F32),
            pltpu.VMEM((nchunks, N, L), BF16),
            pltpu.VMEM((seq, N), BF16),
            pltpu.VMEM((nchunks, 4 * SSM_HPG, L), F32),
            pltpu.VMEM((seq, L), F32),
            pltpu.VMEM((seq, GW), F32),
            pltpu.VMEM((N, GW), F32),
        ],
        compiler_params=_params("parallel", "parallel"),
        name="ssd",
    )(proj3, proj3, proj3, proj3, dt5, conv_w, conv_w, conv_w, conv_b, conv_b, conv_b, dtb, alog, dsk, ng)


def _rms_rope(x, g, cos, sin):
    ms = jnp.mean(x * x, axis=-1, keepdims=True)
    x = x * lax.rsqrt(ms + EPS) * g
    lane = lax.broadcasted_iota(jnp.int32, x.shape, 1)
    partner = jnp.where(lane % 2 == 0, pltpu.roll(x, ATTN_HEAD_DIM - 1, axis=1), pltpu.roll(x, 1, axis=1))
    return x * cos + partner * sin


def _attn_kernel(q_ref, k_ref, v_ref, cos_ref, sin_ref, gq_ref, gk_ref, o_ref,
                 k_scr, m_scr, l_scr, acc_scr, *, tq, tk, rows):
    seq = k_ref.shape[0]
    hd = ATTN_HEAD_DIM
    qi = pl.program_id(2)

    @pl.when(qi == 0)
    def _():
        def body(r, _):
            sl = pl.ds(pl.multiple_of(r * rows, rows), rows)
            k = _rms_rope(k_ref[sl, :].astype(F32), gk_ref[...], cos_ref[sl, :], sin_ref[sl, :])
            k_scr[sl, :] = k.astype(BF16)
            return 0

        lax.fori_loop(0, seq // rows, body, 0)

    q_sl = pl.ds(pl.multiple_of(qi * tq, tq), tq)
    cos_q = cos_ref[q_sl, :]
    sin_q = sin_ref[q_sl, :]
    scale = hd ** -0.5
    qs = []
    for r in range(ATTN_REP):
        q = _rms_rope(q_ref[:, r * hd:(r + 1) * hd].astype(F32), gq_ref[...], cos_q, sin_q)
        qs.append((q * scale).astype(BF16))
    q_all = jnp.concatenate(qs, axis=0)

    m_scr[...] = jnp.full_like(m_scr, -jnp.inf)
    l_scr[...] = jnp.zeros_like(l_scr)
    acc_scr[...] = jnp.zeros_like(acc_scr)

    def body(j, _):
        sl = pl.ds(pl.multiple_of(j * tk, tk), tk)
        s = lax.dot_general(q_all, k_scr[sl, :], (((1,), (1,)), ((), ())), preferred_element_type=F32)
        m_old = m_scr[...]
        m_new = jnp.maximum(m_old, jnp.max(s, axis=-1, keepdims=True))
        alpha = jnp.exp(m_old - m_new)
        p = jnp.exp(s - m_new)
        l_scr[...] = alpha * l_scr[...] + jnp.sum(p, axis=-1, keepdims=True)
        acc_scr[...] = alpha * acc_scr[...] + jnp.dot(p.astype(BF16), v_ref[sl, :], preferred_element_type=F32)
        m_scr[...] = m_new
        return 0

    lax.fori_loop(0, seq // tk, body, 0)

    out = acc_scr[...] / l_scr[...]
    for r in range(ATTN_REP):
        o_ref[:, r * hd:(r + 1) * hd] = out[r * tq:(r + 1) * tq].astype(o_ref.dtype)


def _attn(proj3, cos, sin, gq, gk, *, tq, tk):
    bsz, seq, _ = proj3.shape
    hd = ATTN_HEAD_DIM
    qw = ATTN_REP * hd
    return pl.pallas_call(
        functools.partial(_attn_kernel, tq=tq, tk=tk, rows=min(seq, 512)),
        grid=(bsz, ATTN_KV_HEADS, seq // tq),
        in_specs=[
            pl.BlockSpec((None, tq, qw), lambda b, h, i: (b, i, COL_Q // qw + h)),
            pl.BlockSpec((None, seq, hd), lambda b, h, i: (b, 0, COL_K // hd + h)),
            pl.BlockSpec((None, seq, hd), lambda b, h, i: (b, 0, COL_V // hd + h)),
            pl.BlockSpec((seq, hd), lambda b, h, i: (0, 0)),
            pl.BlockSpec((seq, hd), lambda b, h, i: (0, 0)),
            pl.BlockSpec((1, hd), lambda b, h, i: (0, 0)),
            pl.BlockSpec((1, hd), lambda b, h, i: (0, 0)),
        ],
        out_specs=pl.BlockSpec((None, tq, qw), lambda b, h, i: (b, i, h)),
        out_shape=jax.ShapeDtypeStruct((bsz, seq, ATTN_Q_HEADS * hd), BF16),
        scratch_shapes=[
            pltpu.VMEM((seq, hd), BF16),
            pltpu.VMEM((ATTN_REP * tq, 1), F32),
            pltpu.VMEM((ATTN_REP * tq, 1), F32),
            pltpu.VMEM((ATTN_REP * tq, hd), F32),
        ],
        compiler_params=_params("parallel", "parallel", "arbitrary"),
        name="attn",
    )(proj3, proj3, proj3, cos, sin, gq, gk)


def _merge_kernel(ys_ref, ya_ref, wa_ref, wb_ref, ga_ref, gb_ref, o_ref):
    a = jnp.dot(ys_ref[...], wa_ref[...], preferred_element_type=F32)
    b = jnp.dot(ya_ref[...], wb_ref[...], preferred_element_type=F32)
    merged = _sigmoid(ga_ref[...].astype(F32)) * a + _sigmoid(gb_ref[...].astype(F32)) * b
    o_ref[...] = merged.astype(o_ref.dtype)


def _merge(ys, ya, wa, wb, proj, *, tm, tn):
    m, d = ys.shape
    return pl.pallas_call(
        _merge_kernel,
        grid=(m // tm, D_MODEL // tn),
        in_specs=[
            pl.BlockSpec((tm, d), lambda i, j: (i, 0)),
            pl.BlockSpec((tm, d), lambda i, j: (i, 0)),
            pl.BlockSpec((d, tn), lambda i, j: (0, j)),
            pl.BlockSpec((d, tn), lambda i, j: (0, j)),
            pl.BlockSpec((tm, tn), lambda i, j: (i, COL_GA // tn + j)),
            pl.BlockSpec((tm, tn), lambda i, j: (i, COL_GB // tn + j)),
        ],
        out_specs=pl.BlockSpec((tm, tn), lambda i, j: (i, j)),
        out_shape=jax.ShapeDtypeStruct((m, D_MODEL), BF16),
        compiler_params=_params("parallel", "arbitrary"),
        name="merge",
    )(ys, ya, wa, wb, proj, proj)


def _outp_kernel(mg_ref, wo_ref, x_ref, g_ref, x1_ref, h2_ref):
    x1 = x_ref[...] + jnp.dot(mg_ref[...], wo_ref[...], preferred_element_type=F32)
    x1_ref[...] = x1
    ms = jnp.mean(x1 * x1, axis=-1, keepdims=True)
    h2_ref[...] = (x1 * lax.rsqrt(ms + EPS) * g_ref[...]).astype(h2_ref.dtype)


def _outp(merged, wo, x2d, g, *, tm):
    m, d = x2d.shape
    return pl.pallas_call(
        _outp_kernel,
        grid=(m // tm,),
        in_specs=[
            pl.BlockSpec((tm, d), lambda i: (i, 0)),
            pl.BlockSpec((d, d), lambda i: (0, 0)),
            pl.BlockSpec((tm, d), lambda i: (i, 0)),
            pl.BlockSpec((1, d), lambda i: (0, 0)),
        ],
        out_specs=[
            pl.BlockSpec((tm, d), lambda i: (i, 0)),
            pl.BlockSpec((tm, d), lambda i: (i, 0)),
        ],
        out_shape=[
            jax.ShapeDtypeStruct((m, d), F32),
            jax.ShapeDtypeStruct((m, d), BF16),
        ],
        compiler_params=_params("parallel"),
        name="outp",
    )(merged, wo, x2d, g)


def _mlp_kernel(h2_ref, w1_ref, w2_ref, x1_ref, g_ref, o_ref, *, final_norm):
    f = pl.program_id(1)

    @pl.when(f == 0)
    def _():
        o_ref[...] = x1_ref[...]

    act = jnp.maximum(jnp.dot(h2_ref[...], w1_ref[...], preferred_element_type=F32), 0.0)
    o_ref[...] += jnp.dot((act * act).astype(BF16), w2_ref[...], preferred_element_type=F32)

    if final_norm:
        @pl.when(f == pl.num_programs(1) - 1)
        def _():
            y = o_ref[...]
            ms = jnp.mean(y * y, axis=-1, keepdims=True)
            o_ref[...] = y * lax.rsqrt(ms + EPS) * g_ref[...]


def _mlp(h2, w1, w2, x1, g, *, tm, tf, final_norm):
    m, d = x1.shape
    ff = w1.shape[1]
    return pl.pallas_call(
        functools.partial(_mlp_kernel, final_norm=final_norm),
        grid=(m // tm, ff // tf),
        in_specs=[
            pl.BlockSpec((tm, d), lambda i, f: (i, 0)),
            pl.BlockSpec((d, tf), lambda i, f: (0, f)),
            pl.BlockSpec((tf, d), lambda i, f: (f, 0)),
            pl.BlockSpec((tm, d), lambda i, f: (i, 0)),
            pl.BlockSpec((1, d), lambda i, f: (0, 0)),
        ],
        out_specs=pl.BlockSpec((tm, d), lambda i, f: (i, 0)),
        out_shape=jax.ShapeDtypeStruct((m, d), F32),
        compiler_params=_params("parallel", "arbitrary"),
        name="mlp",
    )(h2, w1, w2, x1, g)


def _rope_tables(seq):
    rows = seq // GRID_W
    row = jnp.repeat(jnp.arange(rows, dtype=F32), GRID_W)
    col = jnp.tile(jnp.arange(GRID_W, dtype=F32), rows)
    half = ATTN_HEAD_DIM // 2
    inv_freq = 1.0 / (ROPE_THETA ** (jnp.arange(0, half, 2, dtype=F32) / half))
    ang = jnp.concatenate([row[:, None] * inv_freq, col[:, None] * inv_freq], axis=-1)
    cos = jnp.repeat(jnp.cos(ang), 2, axis=-1)
    sin = jnp.repeat(jnp.sin(ang), 2, axis=-1) * jnp.tile(jnp.array([-1.0, 1.0], F32), half)
    return cos, sin


def _group_rows(v_f, v_b):
    g = jnp.concatenate([v_f.reshape(SSM_GROUPS, SSM_HPG), v_b.reshape(SSM_GROUPS, SSM_HPG)], axis=1)
    return g[:, :, None].astype(F32)


def kernel(x, g_mix, w_in, conv_w, conv_b, dt_bias_f, dt_bias_b, a_log_f, a_log_b, d_skip, ssm_norm_g,
           q_norm_g, k_norm_g, w_ssm_up, w_attn_up, w_out, g_mlp, w_mlp_in, w_mlp_out, g_final):
    bsz, seq, d = x.shape
    m = bsz * seq
    depth = w_in.shape[0]
    cos, sin = _rope_tables(seq)
    dt_lo = SSM_D_INNER + SSM_XBC
    dt_hi = dt_lo + 2 * SSM_HEADS
    nchunks = seq // SSM_CHUNK
    tm_big = min(m, 1024)
    tm_mid = min(m, 512)

    x2d = x.reshape(m, d)
    for layer in range(depth):
        w = w_in[layer]
        w_main = jnp.concatenate([w[:, :dt_lo], w[:, dt_hi:]], axis=1).astype(BF16)
        w_dt = jnp.pad(w[:, dt_lo:dt_hi], ((0, 0), (0, DT_PAD - 2 * SSM_HEADS))).astype(BF16)
        proj, dt = _proj(x2d, g_mix[layer][None, :], w_main, w_dt, tm=tm_big, tn=1024)
        proj3 = proj.reshape(bsz, seq, N_MAIN)

        dt5 = dt[:, :2 * SSM_HEADS].reshape(bsz, nchunks, SSM_CHUNK, 2, SSM_GROUPS, SSM_HPG)
        dt5 = dt5.transpose(0, 4, 1, 3, 5, 2).reshape(bsz, SSM_GROUPS, nchunks, 2 * SSM_HPG, SSM_CHUNK)
        y_ssm = _ssd(
            proj3, dt5, conv_w[layer], conv_b[layer][None, :],
            _group_rows(dt_bias_f[layer], dt_bias_b[layer]),
            _group_rows(a_log_f[layer], a_log_b[layer]),
            jnp.repeat(d_skip[layer].astype(F32), SSM_HEAD_DIM)[None, :],
            ssm_norm_g[layer][None, :])

        y_attn = _attn(proj3, cos, sin, q_norm_g[layer][None, :], k_norm_g[layer][None, :],
                       tq=min(seq, 128), tk=min(seq, 1024))

        merged = _merge(y_ssm.reshape(m, SSM_D_INNER), y_attn.reshape(m, ATTN_Q_HEADS * ATTN_HEAD_DIM),
                        w_ssm_up[layer].astype(BF16), w_attn_up[layer].astype(BF16), proj,
                        tm=tm_big, tn=512)
        x1, h2 = _outp(merged, w_out[layer].astype(BF16), x2d, g_mlp[layer][None, :], tm=tm_mid)
        x2d = _mlp(h2, w_mlp_in[layer].astype(BF16), w_mlp_out[layer].astype(BF16), x1, g_final[None, :],
                   tm=tm_mid, tf=512, final_norm=layer == depth - 1)
    return x2d.reshape(bsz, seq, d)
```

```python
import functools

import jax
import jax.numpy as jnp
import numpy as np
from jax import lax
from jax.experimental import pallas as pl
from jax.experimental.pallas import tpu as pltpu

F32 = jnp.float32
BF16 = jnp.bfloat16

EPS = 1e-6
LOG2E = 1.4426950408889634
D_MODEL = 2048
SSM_D_INNER = 2048
SSM_HEAD_DIM = 64
SSM_HEADS = SSM_D_INNER // SSM_HEAD_DIM
SSM_GROUPS = 8
SSM_HPG = SSM_HEADS // SSM_GROUPS
SSM_GW = SSM_D_INNER // SSM_GROUPS
SSM_STATE = 128
SSM_CONV_K = 5
SSM_CHUNK = 128
SSM_XBC = SSM_D_INNER + 2 * SSM_GROUPS * SSM_STATE
ATTN_HEAD_DIM = 128
ATTN_Q_HEADS = 16
ATTN_KV_HEADS = 4
ATTN_REP = ATTN_Q_HEADS // ATTN_KV_HEADS
ROPE_THETA = 10000.0
GRID_W = 64
D_FF = 4 * D_MODEL

COL_Z = 0
COL_XS = COL_Z + SSM_D_INNER
COL_B = COL_XS + SSM_D_INNER
COL_C = COL_B + SSM_GROUPS * SSM_STATE
COL_Q = COL_C + SSM_GROUPS * SSM_STATE
COL_K = COL_Q + ATTN_Q_HEADS * ATTN_HEAD_DIM
COL_V = COL_K + ATTN_KV_HEADS * ATTN_HEAD_DIM
COL_GA = COL_V + ATTN_KV_HEADS * ATTN_HEAD_DIM
COL_GB = COL_GA + D_MODEL
N_MAIN = COL_GB + D_MODEL
DT_PAD = 128

V7X_VMEM_LIMIT = 56 * 1024 * 1024
CONV_HALO = 16
CONV_WIN = SSM_CHUNK + 2 * CONV_HALO


def _params(*sem):
    return pltpu.CompilerParams(dimension_semantics=sem, vmem_limit_bytes=V7X_VMEM_LIMIT)


def _sigmoid(x):
    return 1.0 / (1.0 + jnp.exp(-x))


def _softplus(x):
    return jnp.maximum(x, 0.0) + jnp.log(1.0 + jnp.exp(-jnp.abs(x)))


def _proj_kernel(x_ref, g_ref, w_ref, wdt_ref, o_ref, dt_ref, h_scr, *, rows):
    @pl.when(pl.program_id(1) == 0)
    def _():
        def body(r, _):
            sl = pl.ds(pl.multiple_of(r * rows, rows), rows)
            x = x_ref[sl, :]
            ms = jnp.mean(x * x, axis=-1, keepdims=True)
            hb = (x * lax.rsqrt(ms + EPS) * g_ref[...]).astype(BF16)
            h_scr[sl, :] = hb
            dt_ref[sl, :] = jnp.dot(hb, wdt_ref[...], preferred_element_type=F32)
            return 0

        lax.fori_loop(0, x_ref.shape[0] // rows, body, 0)

    o_ref[...] = jnp.dot(h_scr[...], w_ref[...], preferred_element_type=F32).astype(o_ref.dtype)


def _proj(x2d, g, w_main, w_dt, *, tm, tn):
    m, d = x2d.shape
    n = w_main.shape[1]
    return pl.pallas_call(
        functools.partial(_proj_kernel, rows=min(tm, 256)),
        grid=(m // tm, n // tn),
        in_specs=[
            pl.BlockSpec((tm, d), lambda i, j: (i, 0)),
            pl.BlockSpec((1, d), lambda i, j: (0, 0)),
            pl.BlockSpec((d, tn), lambda i, j: (0, j)),
            pl.BlockSpec((d, DT_PAD), lambda i, j: (0, 0)),
        ],
        out_specs=[
            pl.BlockSpec((tm, tn), lambda i, j: (i, j)),
            pl.BlockSpec((tm, DT_PAD), lambda i, j: (i, 0)),
        ],
        out_shape=[
            jax.ShapeDtypeStruct((m, n), BF16),
            jax.ShapeDtypeStruct((m, DT_PAD), F32),
        ],
        scratch_shapes=[pltpu.VMEM((tm, d), BF16)],
        compiler_params=_params("parallel", "arbitrary"),
        name="proj",
    )(x2d, g, w_main, w_dt)


def _conv_silu_chunk(u_ref, w_ref, b_ref, c, seq):
    L = SSM_CHUNK
    start = jnp.clip(c * L - CONV_HALO, 0, seq - CONV_WIN)
    start = pl.multiple_of(start, CONV_HALO)
    off = c * L - start
    u = u_ref[pl.ds(start, CONV_WIN), :]
    t_idx = lax.broadcasted_iota(jnp.int32, (L, CONV_WIN), 0)
    j_idx = lax.broadcasted_iota(jnp.int32, (L, CONV_WIN), 1)
    base = j_idx - t_idx - off
    shifts = [jnp.where(base == k - SSM_CONV_K // 2, 1.0, 0.0).astype(BF16) for k in range(SSM_CONV_K)]
    r = jnp.dot(jnp.concatenate(shifts, axis=0), u, preferred_element_type=F32)
    acc = r[0:L] * w_ref[0:1, :] + b_ref[...]
    for k in range(1, SSM_CONV_K):
        acc = acc + r[k * L:(k + 1) * L] * w_ref[k:k + 1, :]
    return acc * _sigmoid(acc)


def _split3(a):
    a1 = a.astype(BF16)
    r1 = a - a1.astype(F32)
    a2 = r1.astype(BF16)
    a3 = (r1 - a2.astype(F32)).astype(BF16)
    return a1, a2, a3


def _ssd_kernel(z_ref, xs_ref, b_ref, c_ref, dt_ref, cwx_ref, cwb_ref, cwc_ref, cbx_ref, cbb_ref,
                cbc_ref, dtb_ref, alog_ref, dsk_ref, ng_ref, y_ref,
                x_scr, bt_scr, c_scr, rows_scr, cols_scr, yacc_scr, h_scr):
    L = SSM_CHUNK
    H = SSM_HPG
    seq = xs_ref.shape[0]
    nchunks = seq // L

    s_i = lax.broadcasted_iota(jnp.int32, (L, L), 0)
    t_i = lax.broadcasted_iota(jnp.int32, (L, L), 1)
    tri_pre = jnp.where(s_i <= t_i, 1.0, 0.0).astype(BF16)
    tri_suf = jnp.where(s_i >= t_i, 1.0, 0.0).astype(BF16)
    neg_a = -jnp.exp(alog_ref[...])

    def prep(c, _):
        rows_sl = pl.ds(pl.multiple_of(c * L, L), L)
        x_scr[rows_sl, :] = _conv_silu_chunk(xs_ref, cwx_ref, cbx_ref, c, seq)
        bm = _conv_silu_chunk(b_ref, cwb_ref, cbb_ref, c, seq)
        bt_scr[c] = bm.T.astype(BF16)
        c_scr[rows_sl, :] = _conv_silu_chunk(c_ref, cwc_ref, cbc_ref, c, seq).astype(BF16)

        dt = _softplus(dt_ref[c] + dtb_ref[...])
        a = neg_a * dt
        a3 = jnp.concatenate(_split3(a), axis=0)
        pre = jnp.dot(a3, tri_pre, preferred_element_type=F32)
        suf = jnp.dot(a3, tri_suf, preferred_element_type=F32)
        pre = pre[0:2 * H] + pre[2 * H:4 * H] + pre[4 * H:6 * H]
        suf = suf[0:2 * H] + suf[2 * H:4 * H] + suf[4 * H:6 * H]
        is_fwd = lax.broadcasted_iota(jnp.int32, (2 * H, L), 0) < H
        rows = jnp.concatenate([jnp.where(is_fwd, pre, suf), dt], axis=0)
        rows_scr[c] = rows
        padded = jnp.concatenate([rows, jnp.zeros((L - 4 * H, L), F32)], axis=0)
        cols_scr[rows_sl, :] = padded.T
        return 0

    lax.fori_loop(0, nchunks, prep, 0)

    lane128 = lax.broadcasted_iota(jnp.int32, (L, L), 1)
    row_h = lax.broadcasted_iota(jnp.int32, (H * L, SSM_GW), 0) // L
    lane_h = lax.broadcasted_iota(jnp.int32, (H * L, SSM_GW), 1) // SSM_HEAD_DIM
    t_w = lax.broadcasted_iota(jnp.int32, (L, H * L), 0)
    s_w = lax.broadcasted_iota(jnp.int32, (L, H * L), 1) % L

    def col_bcast(cols, lane):
        return jnp.broadcast_to(cols[:, lane:lane + 1], (L, L))

    def expand(bc):
        lo = jnp.where(lane128 < SSM_HEAD_DIM, bc[0], bc[1])
        hi = jnp.where(lane128 < SSM_HEAD_DIM, bc[2], bc[3])
        return jnp.concatenate([lo, hi], axis=1)

    def chunk(c, d):
        rows_sl = pl.ds(pl.multiple_of(c * L, L), L)
        rows = rows_scr[c]
        cols = cols_scr[rows_sl, :]
        xc = x_scr[rows_sl, :]
        cm = c_scr[rows_sl, :]
        bt = bt_scr[c]
        scores = jnp.dot(cm, bt, preferred_element_type=F32)
        cum_bc = [col_bcast(cols, d * H + h) for h in range(H)]
        dt_bc = [col_bcast(cols, 2 * H + d * H + h) for h in range(H)]
        seg = (jnp.concatenate(cum_bc, axis=1)
               - jnp.concatenate([rows[d * H + h:d * H + h + 1, :] for h in range(H)], axis=1))
        keep = (s_w <= t_w) if d == 0 else (s_w >= t_w)
        decay = jnp.exp(jnp.where(keep, seg, -jnp.inf))
        mix = (jnp.concatenate([scores] * H, axis=1) * decay).astype(BF16)
        cum_e = expand(cum_bc)
        xdt = xc * expand(dt_bc)
        xdt_b = xdt.astype(BF16)
        x_bd = jnp.where(row_h == lane_h, jnp.concatenate([xdt_b] * H, axis=0), jnp.zeros((), BF16))
        y = jnp.dot(mix, x_bd, preferred_element_type=F32)
        state = h_scr[...]
        y = y + jnp.dot(cm, state.astype(BF16), preferred_element_type=F32) * jnp.exp(cum_e)
        tot = cum_e[L - 1:L, :] if d == 0 else cum_e[0:1, :]
        xe = (xdt * jnp.exp(tot - cum_e)).astype(BF16)
        h_scr[...] = jnp.exp(tot) * state + jnp.dot(bt, xe, preferred_element_type=F32)
        return y

    h_scr[...] = jnp.zeros_like(h_scr)

    def fwd(c, _):
        rows_sl = pl.ds(pl.multiple_of(c * L, L), L)
        yacc_scr[rows_sl, :] = chunk(c, 0) + dsk_ref[...] * x_scr[rows_sl, :]
        return 0

    lax.fori_loop(0, nchunks, fwd, 0)

    h_scr[...] = jnp.zeros_like(h_scr)

    def bwd(i, _):
        c = nchunks - 1 - i
        rows_sl = pl.ds(pl.multiple_of(c * L, L), L)
        y = yacc_scr[rows_sl, :] + chunk(c, 1)
        z = z_ref[rows_sl, :].astype(F32)
        y = y * (z * _sigmoid(z))
        ms = jnp.mean(y * y, axis=-1, keepdims=True)
        y_ref[rows_sl, :] = (y * lax.rsqrt(ms + EPS) * ng_ref[...]).astype(y_ref.dtype)
        return 0

    lax.fori_loop(0, nchunks, bwd, 0)


def _ssd(proj3, dt5, conv_w, conv_b, dtb, alog, dsk, ng):
    bsz, seq, _ = proj3.shape
    L, G, GW, N = SSM_CHUNK, SSM_GROUPS, SSM_GW, SSM_STATE
    nchunks = seq // L
    k = SSM_CONV_K
    col = lambda base, w: (lambda b, g: (b, 0, base // w + g))
    pcol = lambda base, w: (lambda b, g: (0, base // w + g))
    return pl.pallas_call(
        _ssd_kernel,
        grid=(bsz, G),
        in_specs=[
            pl.BlockSpec((None, seq, GW), col(COL_Z, GW)),
            pl.BlockSpec((None, seq, GW), col(COL_XS, GW)),
            pl.BlockSpec((None, seq, N), col(COL_B, N)),
            pl.BlockSpec((None, seq, N), col(COL_C, N)),
            pl.BlockSpec((None, None, nchunks, 2 * SSM_HPG, L), lambda b, g: (b, g, 0, 0, 0)),
            pl.BlockSpec((k, GW), pcol(0, GW)),
            pl.BlockSpec((k, N), pcol(SSM_D_INNER, N)),
            pl.BlockSpec((k, N), pcol(SSM_D_INNER + G * N, N)),
            pl.BlockSpec((1, GW), pcol(0, GW)),
            pl.BlockSpec((1, N), pcol(SSM_D_INNER, N)),
            pl.BlockSpec((1, N), pcol(SSM_D_INNER + G * N, N)),
            pl.BlockSpec((None, 2 * SSM_HPG, 1), lambda b, g: (g, 0, 0)),
            pl.BlockSpec((None, 2 * SSM_HPG, 1), lambda b, g: (g, 0, 0)),
            pl.BlockSpec((1, GW), pcol(0, GW)),
            pl.BlockSpec((1, GW), pcol(0, GW)),
        ],
        out_specs=pl.BlockSpec((None, seq, GW), lambda b, g: (b, 0, g)),
        out_shape=jax.ShapeDtypeStruct((bsz, seq, SSM_D_INNER), BF16),
        scratch_shapes=[
            pltpu.VMEM((seq, GW), F32),
            pltpu.VMEM((nchunks, N, L), BF16),
            pltpu.VMEM((seq, N), BF16),
            pltpu.VMEM((nchunks, 4 * SSM_HPG, L), F32),
            pltpu.VMEM((seq, L), F32),
            pltpu.VMEM((seq, GW), F32),
            pltpu.VMEM((N, GW), F32),
        ],
        compiler_params=_params("parallel", "parallel"),
        name="ssd",
    )(proj3, proj3, proj3, proj3, dt5, conv_w, conv_w, conv_w, conv_b, conv_b, conv_b, dtb, alog, dsk, ng)


def _rms_rope(x, g, cos, sin):
    ms = jnp.mean(x * x, axis=-1, keepdims=True)
    x = x * lax.rsqrt(ms + EPS) * g
    lane = lax.broadcasted_iota(jnp.int32, x.shape, 1)
    partner = jnp.where(lane % 2 == 0, pltpu.roll(x, ATTN_HEAD_DIM - 1, axis=1), pltpu.roll(x, 1, axis=1))
    return x * cos + partner * sin


def _qkprep_kernel(q_ref, k_ref, cos_ref, sin_ref, gq_ref, gk_ref, qo_ref, ko_ref):
    hd = ATTN_HEAD_DIM
    cos = cos_ref[...]
    sin = sin_ref[...]
    scale = hd ** -0.5 * LOG2E
    for h in range(ATTN_Q_HEADS):
        q = _rms_rope(q_ref[:, h * hd:(h + 1) * hd].astype(F32), gq_ref[...], cos, sin)
        qo_ref[:, h * hd:(h + 1) * hd] = (q * scale).astype(qo_ref.dtype)
    for h in range(ATTN_KV_HEADS):
        k = _rms_rope(k_ref[:, h * hd:(h + 1) * hd].astype(F32), gk_ref[...], cos, sin)
        ko_ref[:, h * hd:(h + 1) * hd] = k.astype(ko_ref.dtype)


def _qkprep(proj, cos, sin, gq, gk, *, seq, tm):
    m = proj.shape[0]
    hd = ATTN_HEAD_DIM
    qw = ATTN_Q_HEADS * hd
    kw = ATTN_KV_HEADS * hd
    pos_blocks = seq // tm
    return pl.pallas_call(
        _qkprep_kernel,
        grid=(m // tm,),
        in_specs=[
            pl.BlockSpec((tm, qw), lambda i: (i, COL_Q // qw)),
            pl.BlockSpec((tm, kw), lambda i: (i, COL_K // kw)),
            pl.BlockSpec((tm, hd), lambda i: (i % pos_blocks, 0)),
            pl.BlockSpec((tm, hd), lambda i: (i % pos_blocks, 0)),
            pl.BlockSpec((1, hd), lambda i: (0, 0)),
            pl.BlockSpec((1, hd), lambda i: (0, 0)),
        ],
        out_specs=[
            pl.BlockSpec((tm, qw), lambda i: (i, 0)),
            pl.BlockSpec((tm, kw), lambda i: (i, 0)),
        ],
        out_shape=[
            jax.ShapeDtypeStruct((m, qw), BF16),
            jax.ShapeDtypeStruct((m, kw), BF16),
        ],
        compiler_params=_params("parallel"),
        name="qkprep",
    )(proj, proj, cos, sin, gq, gk)


def _attn_kernel(q_ref, k_ref, v_ref, o_ref, vx_scr, s_scr, *, tq, tk):
    seq = k_ref.shape[0]
    hd = ATTN_HEAD_DIM
    nrow = ATTN_REP * tq

    @pl.when(pl.program_id(2) == 0)
    def _():
        vx_scr[:, :hd] = v_ref[...]
        vx_scr[:, hd:] = jnp.ones((seq, hd), BF16)

    q_all = jnp.concatenate([q_ref[:, r * hd:(r + 1) * hd] for r in range(ATTN_REP)], axis=0)

    m_acc = jnp.full((nrow, hd), -jnp.inf, F32)
    for kb in range(seq // tk):
        s = lax.dot_general(q_all, k_ref[kb * tk:(kb + 1) * tk, :], (((1,), (1,)), ((), ())),
                            preferred_element_type=F32)
        s_scr[:, kb * tk:(kb + 1) * tk] = s
        for lt in range(tk // hd):
            m_acc = jnp.maximum(m_acc, s[:, lt * hd:(lt + 1) * hd])
    m = jnp.max(m_acc, axis=-1, keepdims=True)

    half = nrow // 2
    acc_lo = jnp.zeros((half, 2 * hd), F32)
    acc_hi = jnp.zeros((half, 2 * hd), F32)
    for kb in range(seq // tk):
        pb = jnp.exp2(s_scr[:, kb * tk:(kb + 1) * tk] - m).astype(BF16)
        vb = vx_scr[kb * tk:(kb + 1) * tk, :]
        acc_lo = acc_lo + jnp.dot(pb[:half], vb, preferred_element_type=F32)
        acc_hi = acc_hi + jnp.dot(pb[half:], vb, preferred_element_type=F32)
    acc = jnp.concatenate([acc_lo, acc_hi], axis=0)
    out = acc[:, :hd] / acc[:, hd:]
    for r in range(ATTN_REP):
        o_ref[:, r * hd:(r + 1) * hd] = out[r * tq:(r + 1) * tq].astype(o_ref.dtype)


def _attn(q3, k3, proj3, *, tq, tk):
    bsz, seq, _ = proj3.shape
    hd = ATTN_HEAD_DIM
    qw = ATTN_REP * hd
    return pl.pallas_call(
        functools.partial(_attn_kernel, tq=tq, tk=tk),
        grid=(bsz, ATTN_KV_HEADS, seq // tq),
        in_specs=[
            pl.BlockSpec((None, tq, qw), lambda b, h, i: (b, i, h)),
            pl.BlockSpec((None, seq, hd), lambda b, h, i: (b, 0, h)),
            pl.BlockSpec((None, seq, hd), lambda b, h, i: (b, 0, COL_V // hd + h)),
        ],
        out_specs=pl.BlockSpec((None, tq, qw), lambda b, h, i: (b, i, h)),
        out_shape=jax.ShapeDtypeStruct((bsz, seq, ATTN_Q_HEADS * hd), BF16),
        scratch_shapes=[
            pltpu.VMEM((seq, 2 * hd), BF16),
            pltpu.VMEM((ATTN_REP * tq, seq), F32),
        ],
        compiler_params=_params("parallel", "parallel", "arbitrary"),
        name="attn",
    )(q3, k3, proj3)


def _merge_kernel(ys_ref, ya_ref, wa_ref, wb_ref, ga_ref, gb_ref, o_ref):
    a = jnp.dot(ys_ref[...], wa_ref[...], preferred_element_type=F32)
    b = jnp.dot(ya_ref[...], wb_ref[...], preferred_element_type=F32)
    merged = _sigmoid(ga_ref[...].astype(F32)) * a + _sigmoid(gb_ref[...].astype(F32)) * b
    o_ref[...] = merged.astype(o_ref.dtype)


def _merge(ys, ya, wa, wb, proj, *, tm, tn):
    m, d = ys.shape
    return pl.pallas_call(
        _merge_kernel,
        grid=(m // tm, D_MODEL // tn),
        in_specs=[
            pl.BlockSpec((tm, d), lambda i, j: (i, 0)),
            pl.BlockSpec((tm, d), lambda i, j: (i, 0)),
            pl.BlockSpec((d, tn), lambda i, j: (0, j)),
            pl.BlockSpec((d, tn), lambda i, j: (0, j)),
            pl.BlockSpec((tm, tn), lambda i, j: (i, COL_GA // tn + j)),
            pl.BlockSpec((tm, tn), lambda i, j: (i, COL_GB // tn + j)),
        ],
        out_specs=pl.BlockSpec((tm, tn), lambda i, j: (i, j)),
        out_shape=jax.ShapeDtypeStruct((m, D_MODEL), BF16),
        compiler_params=_params("parallel", "arbitrary"),
        name="merge",
    )(ys, ya, wa, wb, proj, proj)


def _outp_kernel(mg_ref, wo_ref, x_ref, g_ref, x1_ref, h2_ref):
    x1 = x_ref[...] + jnp.dot(mg_ref[...], wo_ref[...], preferred_element_type=F32)
    x1_ref[...] = x1
    ms = jnp.mean(x1 * x1, axis=-1, keepdims=True)
    h2_ref[...] = (x1 * lax.rsqrt(ms + EPS) * g_ref[...]).astype(h2_ref.dtype)


def _outp(merged, wo, x2d, g, *, tm):
    m, d = x2d.shape
    return pl.pallas_call(
        _outp_kernel,
        grid=(m // tm,),
        in_specs=[
            pl.BlockSpec((tm, d), lambda i: (i, 0)),
            pl.BlockSpec((d, d), lambda i: (0, 0)),
            pl.BlockSpec((tm, d), lambda i: (i, 0)),
            pl.BlockSpec((1, d), lambda i: (0, 0)),
        ],
        out_specs=[
            pl.BlockSpec((tm, d), lambda i: (i, 0)),
            pl.BlockSpec((tm, d), lambda i: (i, 0)),
        ],
        out_shape=[
            jax.ShapeDtypeStruct((m, d), F32),
            jax.ShapeDtypeStruct((m, d), BF16),
        ],
        compiler_params=_params("parallel"),
        name="outp",
    )(merged, wo, x2d, g)


def _mlp_kernel(h2_ref, w1_ref, w2_ref, x1_ref, g_ref, o_ref, *, final_norm):
    f = pl.program_id(1)

    @pl.when(f == 0)
    def _():
        o_ref[...] = x1_ref[...]

    act = jnp.maximum(jnp.dot(h2_ref[...], w1_ref[...], preferred_element_type=F32), 0.0)
    o_ref[...] += jnp.dot((act * act).astype(BF16), w2_ref[...], preferred_element_type=F32)

    if final_norm:
        @pl.when(f == pl.num_programs(1) - 1)
        def _():
            y = o_ref[...]
            ms = jnp.mean(y * y, axis=-1, keepdims=True)
            o_ref[...] = y * lax.rsqrt(ms + EPS) * g_ref[...]


def _mlp(h2, w1, w2, x1, g, *, tm, tf, final_norm):
    m, d = x1.shape
    ff = w1.shape[1]
    return pl.pallas_call(
        functools.partial(_mlp_kernel, final_norm=final_norm),
        grid=(m // tm, ff // tf),
        in_specs=[
            pl.BlockSpec((tm, d), lambda i, f: (i, 0)),
            pl.BlockSpec((d, tf), lambda i, f: (0, f)),
            pl.BlockSpec((tf, d), lambda i, f: (f, 0)),
            pl.BlockSpec((tm, d), lambda i, f: (i, 0)),
            pl.BlockSpec((1, d), lambda i, f: (0, 0)),
        ],
        out_specs=pl.BlockSpec((tm, d), lambda i, f: (i, 0)),
        out_shape=jax.ShapeDtypeStruct((m, d), F32),
        compiler_params=_params("parallel", "arbitrary"),
        name="mlp",
    )(h2, w1, w2, x1, g)


def _rope_tables(seq):
    rows = seq // GRID_W
    row = jnp.repeat(jnp.arange(rows, dtype=F32), GRID_W)
    col = jnp.tile(jnp.arange(GRID_W, dtype=F32), rows)
    half = ATTN_HEAD_DIM // 2
    inv_freq = 1.0 / (ROPE_THETA ** (jnp.arange(0, half, 2, dtype=F32) / half))
    ang = jnp.concatenate([row[:, None] * inv_freq, col[:, None] * inv_freq], axis=-1)
    cos = jnp.repeat(jnp.cos(ang), 2, axis=-1)
    sin = jnp.repeat(jnp.sin(ang), 2, axis=-1) * jnp.tile(jnp.array([-1.0, 1.0], F32), half)
    return cos, sin


def _group_rows(v_f, v_b):
    g = jnp.concatenate([v_f.reshape(SSM_GROUPS, SSM_HPG), v_b.reshape(SSM_GROUPS, SSM_HPG)], axis=1)
    return g[:, :, None].astype(F32)


def kernel(x, g_mix, w_in, conv_w, conv_b, dt_bias_f, dt_bias_b, a_log_f, a_log_b, d_skip, ssm_norm_g,
           q_norm_g, k_norm_g, w_ssm_up, w_attn_up, w_out, g_mlp, w_mlp_in, w_mlp_out, g_final):
    bsz, seq, d = x.shape
    m = bsz * seq
    depth = w_in.shape[0]
    cos, sin = _rope_tables(seq)
    dt_lo = SSM_D_INNER + SSM_XBC
    dt_hi = dt_lo + 2 * SSM_HEADS
    nchunks = seq // SSM_CHUNK
    tm_big = min(m, 1024)
    tm_mid = min(m, 512)

    x2d = x.reshape(m, d)
    for layer in range(depth):
        w = w_in[layer]
        w_main = jnp.concatenate([w[:, :dt_lo], w[:, dt_hi:]], axis=1).astype(BF16)
        w_dt = jnp.pad(w[:, dt_lo:dt_hi], ((0, 0), (0, DT_PAD - 2 * SSM_HEADS))).astype(BF16)
        proj, dt = _proj(x2d, g_mix[layer][None, :], w_main, w_dt, tm=tm_big, tn=1024)
        proj3 = proj.reshape(bsz, seq, N_MAIN)

        dt5 = dt[:, :2 * SSM_HEADS].reshape(bsz, nchunks, SSM_CHUNK, 2, SSM_GROUPS, SSM_HPG)
        dt5 = dt5.transpose(0, 4, 1, 3, 5, 2).reshape(bsz, SSM_GROUPS, nchunks, 2 * SSM_HPG, SSM_CHUNK)
        y_ssm = _ssd(
            proj3, dt5, conv_w[layer], conv_b[layer][None, :],
            _group_rows(dt_bias_f[layer], dt_bias_b[layer]),
            _group_rows(a_log_f[layer], a_log_b[layer]),
            jnp.repeat(d_skip[layer].astype(F32), SSM_HEAD_DIM)[None, :],
            ssm_norm_g[layer][None, :])

        q_r, k_r = _qkprep(proj, cos, sin, q_norm_g[layer][None, :], k_norm_g[layer][None, :],
                           seq=seq, tm=min(seq, 256))
        y_attn = _attn(q_r.reshape(bsz, seq, -1), k_r.reshape(bsz, seq, -1), proj3,
                       tq=min(seq, 128), tk=min(seq, 512))

        merged = _merge(y_ssm.reshape(m, SSM_D_INNER), y_attn.reshape(m, ATTN_Q_HEADS * ATTN_HEAD_DIM),
                        w_ssm_up[layer].astype(BF16), w_attn_up[layer].astype(BF16), proj,
                        tm=tm_big, tn=512)
        x1, h2 = _outp(merged, w_out[layer].astype(BF16), x2d, g_mlp[layer][None, :], tm=tm_mid)
        x2d = _mlp(h2, w_mlp_in[layer].astype(BF16), w_mlp_out[layer].astype(BF16), x1, g_final[None, :],
                   tm=tm_mid, tf=512, final_norm=layer == depth - 1)
    return x2d.reshape(bsz, seq, d)
```

```python
import functools

import jax
import jax.numpy as jnp
import numpy as np
from jax import lax
from jax.experimental import pallas as pl
from jax.experimental.pallas import tpu as pltpu

F32 = jnp.float32
BF16 = jnp.bfloat16

EPS = 1e-6
LOG2E = 1.4426950408889634
D_MODEL = 2048
SSM_D_INNER = 2048
SSM_HEAD_DIM = 64
SSM_HEADS = SSM_D_INNER // SSM_HEAD_DIM
SSM_GROUPS = 8
SSM_HPG = SSM_HEADS // SSM_GROUPS
SSM_GW = SSM_D_INNER // SSM_GROUPS
SSM_STATE = 128
SSM_CONV_K = 5
SSM_CHUNK = 128
SSM_XBC = SSM_D_INNER + 2 * SSM_GROUPS * SSM_STATE
ATTN_HEAD_DIM = 128
ATTN_Q_HEADS = 16
ATTN_KV_HEADS = 4
ATTN_REP = ATTN_Q_HEADS // ATTN_KV_HEADS
ROPE_THETA = 10000.0
GRID_W = 64
D_FF = 4 * D_MODEL

COL_Z = 0
COL_XS = COL_Z + SSM_D_INNER
COL_B = COL_XS + SSM_D_INNER
COL_C = COL_B + SSM_GROUPS * SSM_STATE
COL_Q = COL_C + SSM_GROUPS * SSM_STATE
COL_K = COL_Q + ATTN_Q_HEADS * ATTN_HEAD_DIM
COL_V = COL_K + ATTN_KV_HEADS * ATTN_HEAD_DIM
COL_GA = COL_V + ATTN_KV_HEADS * ATTN_HEAD_DIM
COL_GB = COL_GA + D_MODEL
N_MAIN = COL_GB + D_MODEL
DT_PAD = 128

V7X_VMEM_LIMIT = 56 * 1024 * 1024
CONV_HALO = 16
CONV_WIN = SSM_CHUNK + 2 * CONV_HALO


def _params(*sem):
    return pltpu.CompilerParams(dimension_semantics=sem, vmem_limit_bytes=V7X_VMEM_LIMIT)


def _sigmoid(x):
    return 1.0 / (1.0 + jnp.exp(-x))


def _softplus(x):
    return jnp.maximum(x, 0.0) + jnp.log(1.0 + jnp.exp(-jnp.abs(x)))


def _proj_kernel(x_ref, g_ref, w_ref, wdt_ref, o_ref, dt_ref, h_scr, *, rows):
    @pl.when(pl.program_id(1) == 0)
    def _():
        def body(r, _):
            sl = pl.ds(pl.multiple_of(r * rows, rows), rows)
            x = x_ref[sl, :]
            ms = jnp.mean(x * x, axis=-1, keepdims=True)
            hb = (x * lax.rsqrt(ms + EPS) * g_ref[...]).astype(BF16)
            h_scr[sl, :] = hb
            dt_ref[sl, :] = jnp.dot(hb, wdt_ref[...], preferred_element_type=F32)
            return 0

        lax.fori_loop(0, x_ref.shape[0] // rows, body, 0)

    o_ref[...] = jnp.dot(h_scr[...], w_ref[...], preferred_element_type=F32).astype(o_ref.dtype)


def _proj(x2d, g, w_main, w_dt, *, tm, tn):
    m, d = x2d.shape
    n = w_main.shape[1]
    return pl.pallas_call(
        functools.partial(_proj_kernel, rows=min(tm, 256)),
        grid=(m // tm, n // tn),
        in_specs=[
            pl.BlockSpec((tm, d), lambda i, j: (i, 0)),
            pl.BlockSpec((1, d), lambda i, j: (0, 0)),
            pl.BlockSpec((d, tn), lambda i, j: (0, j)),
            pl.BlockSpec((d, DT_PAD), lambda i, j: (0, 0)),
        ],
        out_specs=[
            pl.BlockSpec((tm, tn), lambda i, j: (i, j)),
            pl.BlockSpec((tm, DT_PAD), lambda i, j: (i, 0)),
        ],
        out_shape=[
            jax.ShapeDtypeStruct((m, n), BF16),
            jax.ShapeDtypeStruct((m, DT_PAD), F32),
        ],
        scratch_shapes=[pltpu.VMEM((tm, d), BF16)],
        compiler_params=_params("parallel", "arbitrary"),
        name="proj",
    )(x2d, g, w_main, w_dt)


_CONV_SIDE_TAPS = tuple(k for k in range(SSM_CONV_K) if k != SSM_CONV_K // 2)


def _conv_window(c, seq):
    L = SSM_CHUNK
    start = jnp.clip(c * L - CONV_HALO, 0, seq - CONV_WIN)
    start = pl.multiple_of(start, CONV_HALO)
    return start, (c * L - start) // CONV_HALO


def _shift_matrix(halo_units):
    L = SSM_CHUNK
    t_idx = lax.broadcasted_iota(jnp.int32, (L, CONV_WIN), 0)
    j_idx = lax.broadcasted_iota(jnp.int32, (L, CONV_WIN), 1)
    base = j_idx - t_idx - halo_units * CONV_HALO
    shifts = [jnp.where(base == k - SSM_CONV_K // 2, 1.0, 0.0).astype(BF16) for k in _CONV_SIDE_TAPS]
    return jnp.concatenate(shifts, axis=0)


def _conv_silu(shifted, centre, w_ref, b_ref):
    L = SSM_CHUNK
    mid = SSM_CONV_K // 2
    acc = centre * w_ref[mid:mid + 1, :] + b_ref[...]
    for n, k in enumerate(_CONV_SIDE_TAPS):
        acc = acc + shifted[n * L:(n + 1) * L] * w_ref[k:k + 1, :]
    return acc * _sigmoid(acc)


def _split3(a):
    a1 = a.astype(BF16)
    r1 = a - a1.astype(F32)
    a2 = r1.astype(BF16)
    a3 = (r1 - a2.astype(F32)).astype(BF16)
    return a1, a2, a3


def _ssd_kernel(z_ref, xs_ref, b_ref, c_ref, dt_ref, cwx_ref, cwb_ref, cwc_ref, cbx_ref, cbb_ref,
                cbc_ref, dtb_ref, alog_ref, dsk_ref, ng_ref, y_ref,
                shift_scr, x_scr, bt_scr, c_scr, rows_scr, cols_scr, yf_scr, yb_scr, hf_scr, hb_scr):
    L = SSM_CHUNK
    H = SSM_HPG
    N = SSM_STATE
    seq = xs_ref.shape[0]
    nchunks = seq // L

    row_i = lax.broadcasted_iota(jnp.int32, (L, L), 0)
    lane_i = lax.broadcasted_iota(jnp.int32, (L, L), 1)
    tri_pre = jnp.where(row_i <= lane_i, 1.0, 0.0).astype(BF16)
    tri_suf = jnp.where(row_i >= lane_i, 1.0, 0.0).astype(BF16)
    neg_a = -jnp.exp(alog_ref[...]) * LOG2E
    lane128 = lax.broadcasted_iota(jnp.int32, (L, L), 1)

    def col_bcast(cols, lane):
        return jnp.broadcast_to(cols[:, lane:lane + 1], (L, L))

    def expand(bc):
        lo = jnp.where(lane128 < SSM_HEAD_DIM, bc[0], bc[1])
        hi = jnp.where(lane128 < SSM_HEAD_DIM, bc[2], bc[3])
        return jnp.concatenate([lo, hi], axis=1)

    for halo_units in range(3):
        shift_scr[halo_units] = _shift_matrix(halo_units)

    def prep(c, _):
        rows_sl = pl.ds(pl.multiple_of(c * L, L), L)
        start, halo_units = _conv_window(c, seq)
        shift = shift_scr[halo_units]
        win = pl.ds(start, CONV_WIN)
        sx = jnp.dot(shift, xs_ref[win, :], preferred_element_type=F32)
        sbc = jnp.dot(shift, jnp.concatenate([b_ref[win, :], c_ref[win, :]], axis=1),
                      preferred_element_type=F32)
        xc = _conv_silu(sx, xs_ref[rows_sl, :].astype(F32), cwx_ref, cbx_ref)
        x_scr[rows_sl, :] = xc
        bm = _conv_silu(sbc[:, :N], b_ref[rows_sl, :].astype(F32), cwb_ref, cbb_ref)
        bt_scr[c] = bm.T.astype(BF16)
        c_scr[rows_sl, :] = _conv_silu(sbc[:, N:], c_ref[rows_sl, :].astype(F32), cwc_ref, cbc_ref).astype(BF16)

        dt = _softplus(dt_ref[c] + dtb_ref[...])
        a = neg_a * dt
        a3 = jnp.concatenate(_split3(a), axis=0)
        pre = jnp.dot(a3, tri_pre, preferred_element_type=F32)
        suf = jnp.dot(a3, tri_suf, preferred_element_type=F32)
        pre = pre[0:2 * H] + pre[2 * H:4 * H] + pre[4 * H:6 * H]
        suf = suf[0:2 * H] + suf[2 * H:4 * H] + suf[4 * H:6 * H]
        is_fwd = lax.broadcasted_iota(jnp.int32, (2 * H, L), 0) < H
        rows = jnp.concatenate([jnp.where(is_fwd, pre, suf), dt], axis=0)
        rows_scr[c] = rows
        padded = jnp.concatenate([rows, jnp.zeros((L - 4 * H, L), F32)], axis=0)
        cols_scr[rows_sl, :] = padded.T
        return 0

    lax.fori_loop(0, nchunks, prep, 0, unroll=2)

    row_h = lax.broadcasted_iota(jnp.int32, (H * L, SSM_GW), 0) // L
    lane_h = lax.broadcasted_iota(jnp.int32, (H * L, SSM_GW), 1) // SSM_HEAD_DIM

    def chunk(c, d, h_scr):
        rows_sl = pl.ds(pl.multiple_of(c * L, L), L)
        rows = rows_scr[c]
        cols = cols_scr[rows_sl, :]
        xc = x_scr[rows_sl, :]
        cm = c_scr[rows_sl, :]
        bt = bt_scr[c]
        scores = jnp.dot(cm, bt, preferred_element_type=F32)
        scores = jnp.where((lane_i <= row_i) if d == 0 else (lane_i >= row_i), scores, 0.0)
        cum_bc = [col_bcast(cols, d * H + h) for h in range(H)]
        dt_bc = [col_bcast(cols, 2 * H + d * H + h) for h in range(H)]
        seg = (jnp.concatenate(cum_bc, axis=1)
               - jnp.concatenate([rows[d * H + h:d * H + h + 1, :] for h in range(H)], axis=1))
        decay = jnp.exp2(jnp.minimum(seg, 0.0))
        mix = (jnp.concatenate([scores] * H, axis=1) * decay).astype(BF16)
        cum_e = expand(cum_bc)
        xdt = xc * expand(dt_bc)
        xdt_b = xdt.astype(BF16)
        x_bd = jnp.where(row_h == lane_h, jnp.concatenate([xdt_b] * H, axis=0), jnp.zeros((), BF16))
        y = jnp.dot(mix, x_bd, preferred_element_type=F32)
        state = h_scr[...]
        y = y + jnp.dot(cm, state.astype(BF16), preferred_element_type=F32) * jnp.exp2(cum_e)
        tot = cum_e[L - 1:L, :] if d == 0 else cum_e[0:1, :]
        xe = (xdt * jnp.exp2(tot - cum_e)).astype(BF16)
        h_scr[...] = jnp.exp2(tot) * state + jnp.dot(bt, xe, preferred_element_type=F32)
        return y

    hf_scr[...] = jnp.zeros_like(hf_scr)
    hb_scr[...] = jnp.zeros_like(hb_scr)

    def scan(i, _):
        j = nchunks - 1 - i
        yf_scr[pl.ds(pl.multiple_of(i * L, L), L), :] = chunk(i, 0, hf_scr)
        yb_scr[pl.ds(pl.multiple_of(j * L, L), L), :] = chunk(j, 1, hb_scr)
        return 0

    lax.fori_loop(0, nchunks, scan, 0, unroll=2)

    def finish(c, _):
        rows_sl = pl.ds(pl.multiple_of(c * L, L), L)
        z = z_ref[rows_sl, :].astype(F32)
        y = yf_scr[rows_sl, :] + yb_scr[rows_sl, :] + dsk_ref[...] * x_scr[rows_sl, :]
        y = y * (z * _sigmoid(z))
        ms = jnp.mean(y * y, axis=-1, keepdims=True)
        y_ref[rows_sl, :] = (y * lax.rsqrt(ms + EPS) * ng_ref[...]).astype(y_ref.dtype)
        return 0

    lax.fori_loop(0, nchunks, finish, 0, unroll=2)


def _ssd(proj3, dt5, conv_w, conv_b, dtb, alog, dsk, ng):
    bsz, seq, _ = proj3.shape
    L, G, GW, N = SSM_CHUNK, SSM_GROUPS, SSM_GW, SSM_STATE
    nchunks = seq // L
    k = SSM_CONV_K
    col = lambda base, w: (lambda b, g: (b, 0, base // w + g))
    pcol = lambda base, w: (lambda b, g: (0, base // w + g))
    return pl.pallas_call(
        _ssd_kernel,
        grid=(bsz, G),
        in_specs=[
            pl.BlockSpec((None, seq, GW), col(COL_Z, GW)),
            pl.BlockSpec((None, seq, GW), col(COL_XS, GW)),
            pl.BlockSpec((None, seq, N), col(COL_B, N)),
            pl.BlockSpec((None, seq, N), col(COL_C, N)),
            pl.BlockSpec((None, None, nchunks, 2 * SSM_HPG, L), lambda b, g: (b, g, 0, 0, 0)),
            pl.BlockSpec((k, GW), pcol(0, GW)),
            pl.BlockSpec((k, N), pcol(SSM_D_INNER, N)),
            pl.BlockSpec((k, N), pcol(SSM_D_INNER + G * N, N)),
            pl.BlockSpec((1, GW), pcol(0, GW)),
            pl.BlockSpec((1, N), pcol(SSM_D_INNER, N)),
            pl.BlockSpec((1, N), pcol(SSM_D_INNER + G * N, N)),
            pl.BlockSpec((None, 2 * SSM_HPG, 1), lambda b, g: (g, 0, 0)),
            pl.BlockSpec((None, 2 * SSM_HPG, 1), lambda b, g: (g, 0, 0)),
            pl.BlockSpec((1, GW), pcol(0, GW)),
            pl.BlockSpec((1, GW), pcol(0, GW)),
        ],
        out_specs=pl.BlockSpec((None, seq, GW), lambda b, g: (b, 0, g)),
        out_shape=jax.ShapeDtypeStruct((bsz, seq, SSM_D_INNER), BF16),
        scratch_shapes=[
            pltpu.VMEM((3, len(_CONV_SIDE_TAPS) * L, CONV_WIN), BF16),
            pltpu.VMEM((seq, GW), F32),
            pltpu.VMEM((nchunks, N, L), BF16),
            pltpu.VMEM((seq, N), BF16),
            pltpu.VMEM((nchunks, 4 * SSM_HPG, L), F32),
            pltpu.VMEM((seq, L), F32),
            pltpu.VMEM((seq, GW), F32),
            pltpu.VMEM((seq, GW), F32),
            pltpu.VMEM((N, GW), F32),
            pltpu.VMEM((N, GW), F32),
        ],
        compiler_params=_params("parallel", "parallel"),
        name="ssd",
    )(proj3, proj3, proj3, proj3, dt5, conv_w, conv_w, conv_w, conv_b, conv_b, conv_b, dtb, alog, dsk, ng)


def _rms_rope(x, g, cos, sin):
    ms = jnp.mean(x * x, axis=-1, keepdims=True)
    x = x * lax.rsqrt(ms + EPS) * g
    return x * cos + pltpu.roll(x, ATTN_HEAD_DIM // 2, axis=1) * sin


def _qkprep_kernel(q_ref, k_ref, cos_ref, sin_ref, gq_ref, gk_ref, qo_ref, ko_ref):
    hd = ATTN_HEAD_DIM
    cos = cos_ref[...]
    sin = sin_ref[...]
    scale = hd ** -0.5 * LOG2E
    for h in range(ATTN_Q_HEADS):
        q = _rms_rope(q_ref[:, h * hd:(h + 1) * hd].astype(F32), gq_ref[...], cos, sin)
        qo_ref[:, h * hd:(h + 1) * hd] = (q * scale).astype(qo_ref.dtype)
    for h in range(ATTN_KV_HEADS):
        k = _rms_rope(k_ref[:, h * hd:(h + 1) * hd].astype(F32), gk_ref[...], cos, sin)
        ko_ref[:, h * hd:(h + 1) * hd] = k.astype(ko_ref.dtype)


def _qkprep(proj, cos, sin, gq, gk, *, seq, tm):
    m = proj.shape[0]
    hd = ATTN_HEAD_DIM
    qw = ATTN_Q_HEADS * hd
    kw = ATTN_KV_HEADS * hd
    pos_blocks = seq // tm
    return pl.pallas_call(
        _qkprep_kernel,
        grid=(m // tm,),
        in_specs=[
            pl.BlockSpec((tm, qw), lambda i: (i, COL_Q // qw)),
            pl.BlockSpec((tm, kw), lambda i: (i, COL_K // kw)),
            pl.BlockSpec((tm, hd), lambda i: (i % pos_blocks, 0)),
            pl.BlockSpec((tm, hd), lambda i: (i % pos_blocks, 0)),
            pl.BlockSpec((1, hd), lambda i: (0, 0)),
            pl.BlockSpec((1, hd), lambda i: (0, 0)),
        ],
        out_specs=[
            pl.BlockSpec((tm, qw), lambda i: (i, 0)),
            pl.BlockSpec((tm, kw), lambda i: (i, 0)),
        ],
        out_shape=[
            jax.ShapeDtypeStruct((m, qw), BF16),
            jax.ShapeDtypeStruct((m, kw), BF16),
        ],
        compiler_params=_params("parallel"),
        name="qkprep",
    )(proj, proj, cos, sin, gq, gk)


def _attn_kernel(q_ref, k_ref, v_ref, o_ref, vx_scr, s_scr, *, tq, tk):
    seq = k_ref.shape[0]
    hd = ATTN_HEAD_DIM
    nrow = ATTN_REP * tq

    @pl.when(pl.program_id(2) == 0)
    def _():
        vx_scr[:, :hd] = v_ref[...]
        vx_scr[:, hd:] = jnp.ones((seq, hd), BF16)

    q_all = jnp.concatenate([q_ref[:, r * hd:(r + 1) * hd] for r in range(ATTN_REP)], axis=0)

    m_acc = jnp.full((nrow, hd), -jnp.inf, F32)
    for kb in range(seq // tk):
        s = lax.dot_general(q_all, k_ref[kb * tk:(kb + 1) * tk, :], (((1,), (1,)), ((), ())),
                            preferred_element_type=F32)
        s_scr[:, kb * tk:(kb + 1) * tk] = s
        for lt in range(tk // hd):
            m_acc = jnp.maximum(m_acc, s[:, lt * hd:(lt + 1) * hd])
    m = jnp.max(m_acc, axis=-1, keepdims=True)

    half = nrow // 2
    acc_lo = jnp.zeros((half, 2 * hd), F32)
    acc_hi = jnp.zeros((half, 2 * hd), F32)
    for kb in range(seq // tk):
        pb = jnp.exp2(s_scr[:, kb * tk:(kb + 1) * tk] - m).astype(BF16)
        vb = vx_scr[kb * tk:(kb + 1) * tk, :]
        acc_lo = acc_lo + jnp.dot(pb[:half], vb, preferred_element_type=F32)
        acc_hi = acc_hi + jnp.dot(pb[half:], vb, preferred_element_type=F32)
    acc = jnp.concatenate([acc_lo, acc_hi], axis=0)
    out = acc[:, :hd] / acc[:, hd:]
    for r in range(ATTN_REP):
        o_ref[:, r * hd:(r + 1) * hd] = out[r * tq:(r + 1) * tq].astype(o_ref.dtype)


def _attn(q3, k3, proj3, *, tq, tk):
    bsz, seq, _ = proj3.shape
    hd = ATTN_HEAD_DIM
    qw = ATTN_REP * hd
    return pl.pallas_call(
        functools.partial(_attn_kernel, tq=tq, tk=tk),
        grid=(bsz, ATTN_KV_HEADS, seq // tq),
        in_specs=[
            pl.BlockSpec((None, tq, qw), lambda b, h, i: (b, i, h)),
            pl.BlockSpec((None, seq, hd), lambda b, h, i: (b, 0, h)),
            pl.BlockSpec((None, seq, hd), lambda b, h, i: (b, 0, COL_V // hd + h)),
        ],
        out_specs=pl.BlockSpec((None, tq, qw), lambda b, h, i: (b, i, h)),
        out_shape=jax.ShapeDtypeStruct((bsz, seq, ATTN_Q_HEADS * hd), BF16),
        scratch_shapes=[
            pltpu.VMEM((seq, 2 * hd), BF16),
            pltpu.VMEM((ATTN_REP * tq, seq), F32),
        ],
        compiler_params=_params("parallel", "parallel", "arbitrary"),
        name="attn",
    )(q3, k3, proj3)


def _merge_kernel(ys_ref, ya_ref, wa_ref, wb_ref, ga_ref, gb_ref, o_ref):
    a = jnp.dot(ys_ref[...], wa_ref[...], preferred_element_type=F32)
    b = jnp.dot(ya_ref[...], wb_ref[...], preferred_element_type=F32)
    merged = _sigmoid(ga_ref[...].astype(F32)) * a + _sigmoid(gb_ref[...].astype(F32)) * b
    o_ref[...] = merged.astype(o_ref.dtype)


def _merge(ys, ya, wa, wb, proj, *, tm, tn):
    m, d = ys.shape
    return pl.pallas_call(
        _merge_kernel,
        grid=(m // tm, D_MODEL // tn),
        in_specs=[
            pl.BlockSpec((tm, d), lambda i, j: (i, 0)),
            pl.BlockSpec((tm, d), lambda i, j: (i, 0)),
            pl.BlockSpec((d, tn), lambda i, j: (0, j)),
            pl.BlockSpec((d, tn), lambda i, j: (0, j)),
            pl.BlockSpec((tm, tn), lambda i, j: (i, COL_GA // tn + j)),
            pl.BlockSpec((tm, tn), lambda i, j: (i, COL_GB // tn + j)),
        ],
        out_specs=pl.BlockSpec((tm, tn), lambda i, j: (i, j)),
        out_shape=jax.ShapeDtypeStruct((m, D_MODEL), BF16),
        compiler_params=_params("parallel", "arbitrary"),
        name="merge",
    )(ys, ya, wa, wb, proj, proj)


def _outp_kernel(mg_ref, wo_ref, x_ref, g_ref, x1_ref, h2_ref):
    x1 = x_ref[...] + jnp.dot(mg_ref[...], wo_ref[...], preferred_element_type=F32)
    x1_ref[...] = x1
    ms = jnp.mean(x1 * x1, axis=-1, keepdims=True)
    h2_ref[...] = (x1 * lax.rsqrt(ms + EPS) * g_ref[...]).astype(h2_ref.dtype)


def _outp(merged, wo, x2d, g, *, tm):
    m, d = x2d.shape
    return pl.pallas_call(
        _outp_kernel,
        grid=(m // tm,),
        in_specs=[
            pl.BlockSpec((tm, d), lambda i: (i, 0)),
            pl.BlockSpec((d, d), lambda i: (0, 0)),
            pl.BlockSpec((tm, d), lambda i: (i, 0)),
            pl.BlockSpec((1, d), lambda i: (0, 0)),
        ],
        out_specs=[
            pl.BlockSpec((tm, d), lambda i: (i, 0)),
            pl.BlockSpec((tm, d), lambda i: (i, 0)),
        ],
        out_shape=[
            jax.ShapeDtypeStruct((m, d), F32),
            jax.ShapeDtypeStruct((m, d), BF16),
        ],
        compiler_params=_params("parallel"),
        name="outp",
    )(merged, wo, x2d, g)


def _mlp_kernel(h2_ref, w1_ref, w2_ref, x1_ref, g_ref, o_ref, *, final_norm):
    f = pl.program_id(1)

    @pl.when(f == 0)
    def _():
        o_ref[...] = x1_ref[...]

    act = jnp.maximum(jnp.dot(h2_ref[...], w1_ref[...], preferred_element_type=F32), 0.0)
    o_ref[...] += jnp.dot((act * act).astype(BF16), w2_ref[...], preferred_element_type=F32)

    if final_norm:
        @pl.when(f == pl.num_programs(1) - 1)
        def _():
            y = o_ref[...]
            ms = jnp.mean(y * y, axis=-1, keepdims=True)
            o_ref[...] = y * lax.rsqrt(ms + EPS) * g_ref[...]


def _mlp(h2, w1, w2, x1, g, *, tm, tf, final_norm):
    m, d = x1.shape
    ff = w1.shape[1]
    return pl.pallas_call(
        functools.partial(_mlp_kernel, final_norm=final_norm),
        grid=(m // tm, ff // tf),
        in_specs=[
            pl.BlockSpec((tm, d), lambda i, f: (i, 0)),
            pl.BlockSpec((d, tf), lambda i, f: (0, f)),
            pl.BlockSpec((tf, d), lambda i, f: (f, 0)),
            pl.BlockSpec((tm, d), lambda i, f: (i, 0)),
            pl.BlockSpec((1, d), lambda i, f: (0, 0)),
        ],
        out_specs=pl.BlockSpec((tm, d), lambda i, f: (i, 0)),
        out_shape=jax.ShapeDtypeStruct((m, d), F32),
        compiler_params=_params("parallel", "arbitrary"),
        name="mlp",
    )(h2, w1, w2, x1, g)


def _rope_tables(seq):
    rows = seq // GRID_W
    row = jnp.repeat(jnp.arange(rows, dtype=F32), GRID_W)
    col = jnp.tile(jnp.arange(GRID_W, dtype=F32), rows)
    half = ATTN_HEAD_DIM // 2
    inv_freq = 1.0 / (ROPE_THETA ** (jnp.arange(0, half, 2, dtype=F32) / half))
    ang = jnp.concatenate([row[:, None] * inv_freq, col[:, None] * inv_freq], axis=-1)
    cos = jnp.concatenate([jnp.cos(ang), jnp.cos(ang)], axis=-1)
    sin = jnp.concatenate([-jnp.sin(ang), jnp.sin(ang)], axis=-1)
    return cos, sin


def _deinterleave_heads(a, nheads):
    lead = a.shape[:-1]
    a = a.reshape(lead + (nheads, ATTN_HEAD_DIM // 2, 2))
    return jnp.swapaxes(a, -1, -2).reshape(lead + (nheads * ATTN_HEAD_DIM,))


def _group_rows(v_f, v_b):
    g = jnp.concatenate([v_f.reshape(SSM_GROUPS, SSM_HPG), v_b.reshape(SSM_GROUPS, SSM_HPG)], axis=1)
    return g[:, :, None].astype(F32)


def kernel(x, g_mix, w_in, conv_w, conv_b, dt_bias_f, dt_bias_b, a_log_f, a_log_b, d_skip, ssm_norm_g,
           q_norm_g, k_norm_g, w_ssm_up, w_attn_up, w_out, g_mlp, w_mlp_in, w_mlp_out, g_final):
    bsz, seq, d = x.shape
    m = bsz * seq
    depth = w_in.shape[0]
    cos, sin = _rope_tables(seq)
    dt_lo = SSM_D_INNER + SSM_XBC
    dt_hi = dt_lo + 2 * SSM_HEADS
    nchunks = seq // SSM_CHUNK
    tm_big = min(m, 1024)
    tm_mid = min(m, 512)

    x2d = x.reshape(m, d)
    for layer in range(depth):
        w = w_in[layer]
        q_lo = dt_hi
        k_lo = q_lo + ATTN_Q_HEADS * ATTN_HEAD_DIM
        v_lo = k_lo + ATTN_KV_HEADS * ATTN_HEAD_DIM
        w_main = jnp.concatenate([
            w[:, :dt_lo],
            _deinterleave_heads(w[:, q_lo:k_lo], ATTN_Q_HEADS),
            _deinterleave_heads(w[:, k_lo:v_lo], ATTN_KV_HEADS),
            w[:, v_lo:]], axis=1).astype(BF16)
        w_dt = jnp.pad(w[:, dt_lo:dt_hi], ((0, 0), (0, DT_PAD - 2 * SSM_HEADS))).astype(BF16)
        proj, dt = _proj(x2d, g_mix[layer][None, :], w_main, w_dt, tm=tm_big, tn=1024)
        proj3 = proj.reshape(bsz, seq, N_MAIN)

        dt5 = dt[:, :2 * SSM_HEADS].reshape(bsz, nchunks, SSM_CHUNK, 2, SSM_GROUPS, SSM_HPG)
        dt5 = dt5.transpose(0, 4, 1, 3, 5, 2).reshape(bsz, SSM_GROUPS, nchunks, 2 * SSM_HPG, SSM_CHUNK)
        y_ssm = _ssd(
            proj3, dt5, conv_w[layer], conv_b[layer][None, :],
            _group_rows(dt_bias_f[layer], dt_bias_b[layer]),
            _group_rows(a_log_f[layer], a_log_b[layer]),
            jnp.repeat(d_skip[layer].astype(F32), SSM_HEAD_DIM)[None, :],
            ssm_norm_g[layer][None, :])

        q_r, k_r = _qkprep(proj, cos, sin, _deinterleave_heads(q_norm_g[layer], 1)[None, :],
                           _deinterleave_heads(k_norm_g[layer], 1)[None, :], seq=seq, tm=min(seq, 256))
        y_attn = _attn(q_r.reshape(bsz, seq, -1), k_r.reshape(bsz, seq, -1), proj3,
                       tq=min(seq, 256), tk=min(seq, 512))

        merged = _merge(y_ssm.reshape(m, SSM_D_INNER), y_attn.reshape(m, ATTN_Q_HEADS * ATTN_HEAD_DIM),
                        w_ssm_up[layer].astype(BF16), w_attn_up[layer].astype(BF16), proj,
                        tm=tm_big, tn=512)
        x1, h2 = _outp(merged, w_out[layer].astype(BF16), x2d, g_mlp[layer][None, :], tm=tm_mid)
        x2d = _mlp(h2, w_mlp_in[layer].astype(BF16), w_mlp_out[layer].astype(BF16), x1, g_final[None, :],
                   tm=tm_big, tf=512, final_norm=layer == depth - 1)
    return x2d.reshape(bsz, seq, d)
```

```python
import functools

import jax
import jax.numpy as jnp
import numpy as np
from jax import lax
from jax.experimental import pallas as pl
from jax.experimental.pallas import tpu as pltpu

F32 = jnp.float32
BF16 = jnp.bfloat16

EPS = 1e-6
LOG2E = 1.4426950408889634
D_MODEL = 2048
SSM_D_INNER = 2048
SSM_HEAD_DIM = 64
SSM_HEADS = SSM_D_INNER // SSM_HEAD_DIM
SSM_GROUPS = 8
SSM_HPG = SSM_HEADS // SSM_GROUPS
SSM_GW = SSM_D_INNER // SSM_GROUPS
SSM_STATE = 128
SSM_CONV_K = 5
SSM_CHUNK = 128
SSM_XBC = SSM_D_INNER + 2 * SSM_GROUPS * SSM_STATE
ATTN_HEAD_DIM = 128
ATTN_Q_HEADS = 16
ATTN_KV_HEADS = 4
ATTN_REP = ATTN_Q_HEADS // ATTN_KV_HEADS
ROPE_THETA = 10000.0
GRID_W = 64
D_FF = 4 * D_MODEL

COL_Z = 0
COL_XS = COL_Z + SSM_D_INNER
COL_B = COL_XS + SSM_D_INNER
COL_C = COL_B + SSM_GROUPS * SSM_STATE
COL_Q = COL_C + SSM_GROUPS * SSM_STATE
COL_K = COL_Q + ATTN_Q_HEADS * ATTN_HEAD_DIM
COL_V = COL_K + ATTN_KV_HEADS * ATTN_HEAD_DIM
COL_GA = COL_V + ATTN_KV_HEADS * ATTN_HEAD_DIM
COL_GB = COL_GA + D_MODEL
N_MAIN = COL_GB + D_MODEL
DT_PAD = 128

V7X_VMEM_LIMIT = 56 * 1024 * 1024
CONV_HALO = 16
CONV_WIN = SSM_CHUNK + 2 * CONV_HALO


def _params(*sem):
    return pltpu.CompilerParams(dimension_semantics=sem, vmem_limit_bytes=V7X_VMEM_LIMIT)


def _sigmoid(x):
    return 1.0 / (1.0 + jnp.exp(-x))


def _softplus(x):
    return jnp.maximum(x, 0.0) + jnp.log(1.0 + jnp.exp(-jnp.abs(x)))


def _proj_kernel(x_ref, g_ref, w_ref, wdt_ref, o_ref, dt_ref, h_scr, *, rows):
    @pl.when(pl.program_id(1) == 0)
    def _():
        def body(r, _):
            sl = pl.ds(pl.multiple_of(r * rows, rows), rows)
            x = x_ref[sl, :]
            ms = jnp.mean(x * x, axis=-1, keepdims=True)
            hb = (x * lax.rsqrt(ms + EPS) * g_ref[...]).astype(BF16)
            h_scr[sl, :] = hb
            dt_ref[sl, :] = jnp.dot(hb, wdt_ref[...], preferred_element_type=F32)
            return 0

        lax.fori_loop(0, x_ref.shape[0] // rows, body, 0)

    o_ref[...] = jnp.dot(h_scr[...], w_ref[...], preferred_element_type=F32).astype(o_ref.dtype)


def _proj(x2d, g, w_all, *, tm, tn):
    m, d = x2d.shape
    n = N_MAIN
    return pl.pallas_call(
        functools.partial(_proj_kernel, rows=min(tm, 256)),
        grid=(m // tm, n // tn),
        in_specs=[
            pl.BlockSpec((tm, d), lambda i, j: (i, 0)),
            pl.BlockSpec((1, d), lambda i, j: (0, 0)),
            pl.BlockSpec((d, tn), lambda i, j: (0, j)),
            pl.BlockSpec((d, DT_PAD), lambda i, j: (0, N_MAIN // DT_PAD)),
        ],
        out_specs=[
            pl.BlockSpec((tm, tn), lambda i, j: (i, j)),
            pl.BlockSpec((tm, DT_PAD), lambda i, j: (i, 0)),
        ],
        out_shape=[
            jax.ShapeDtypeStruct((m, n), BF16),
            jax.ShapeDtypeStruct((m, DT_PAD), F32),
        ],
        scratch_shapes=[pltpu.VMEM((tm, d), BF16)],
        compiler_params=_params("parallel", "arbitrary"),
        name="proj",
    )(x2d, g, w_all, w_all)


_CONV_SIDE_TAPS = tuple(k for k in range(SSM_CONV_K) if k != SSM_CONV_K // 2)


def _conv_window(c, seq):
    L = SSM_CHUNK
    start = jnp.clip(c * L - CONV_HALO, 0, seq - CONV_WIN)
    start = pl.multiple_of(start, CONV_HALO)
    return start, (c * L - start) // CONV_HALO


def _shift_matrix(halo_units):
    L = SSM_CHUNK
    t_idx = lax.broadcasted_iota(jnp.int32, (L, CONV_WIN), 0)
    j_idx = lax.broadcasted_iota(jnp.int32, (L, CONV_WIN), 1)
    base = j_idx - t_idx - halo_units * CONV_HALO
    shifts = [jnp.where(base == k - SSM_CONV_K // 2, 1.0, 0.0).astype(BF16) for k in _CONV_SIDE_TAPS]
    return jnp.concatenate(shifts, axis=0)


def _conv_silu(shifted, centre, w_ref, b_ref):
    L = SSM_CHUNK
    mid = SSM_CONV_K // 2
    acc = centre * w_ref[mid:mid + 1, :] + b_ref[...]
    for n, k in enumerate(_CONV_SIDE_TAPS):
        acc = acc + shifted[n * L:(n + 1) * L] * w_ref[k:k + 1, :]
    return acc * _sigmoid(acc)


def _split3(a):
    a1 = a.astype(BF16)
    r1 = a - a1.astype(F32)
    a2 = r1.astype(BF16)
    a3 = (r1 - a2.astype(F32)).astype(BF16)
    return a1, a2, a3


def _ssd_kernel(z_ref, xs_ref, b_ref, c_ref, dt_ref, cwx_ref, cwb_ref, cwc_ref, cbx_ref, cbb_ref,
                cbc_ref, dtb_ref, alog_ref, dsk_ref, ng_ref, y_ref,
                shift_scr, x_scr, bt_scr, c_scr, rows_scr, cols_scr, yf_scr, yb_scr, hf_scr, hb_scr):
    L = SSM_CHUNK
    H = SSM_HPG
    N = SSM_STATE
    seq = xs_ref.shape[0]
    nchunks = seq // L

    row_i = lax.broadcasted_iota(jnp.int32, (L, L), 0)
    lane_i = lax.broadcasted_iota(jnp.int32, (L, L), 1)
    tri_pre = jnp.where(row_i <= lane_i, 1.0, 0.0).astype(BF16)
    tri_suf = jnp.where(row_i >= lane_i, 1.0, 0.0).astype(BF16)
    neg_a = -jnp.exp(alog_ref[...]) * LOG2E
    lane128 = lax.broadcasted_iota(jnp.int32, (L, L), 1)

    def col_bcast(cols, lane):
        return jnp.broadcast_to(cols[:, lane:lane + 1], (L, L))

    def expand(bc):
        lo = jnp.where(lane128 < SSM_HEAD_DIM, bc[0], bc[1])
        hi = jnp.where(lane128 < SSM_HEAD_DIM, bc[2], bc[3])
        return jnp.concatenate([lo, hi], axis=1)

    for halo_units in range(3):
        shift_scr[halo_units] = _shift_matrix(halo_units)

    def prep(c, _):
        rows_sl = pl.ds(pl.multiple_of(c * L, L), L)
        start, halo_units = _conv_window(c, seq)
        shift = shift_scr[halo_units]
        win = pl.ds(start, CONV_WIN)
        sx = jnp.dot(shift, xs_ref[win, :], preferred_element_type=F32)
        sbc = jnp.dot(shift, jnp.concatenate([b_ref[win, :], c_ref[win, :]], axis=1),
                      preferred_element_type=F32)
        xc = _conv_silu(sx, xs_ref[rows_sl, :].astype(F32), cwx_ref, cbx_ref)
        x_scr[rows_sl, :] = xc
        bm = _conv_silu(sbc[:, :N], b_ref[rows_sl, :].astype(F32), cwb_ref, cbb_ref)
        bt_scr[c] = bm.T.astype(BF16)
        c_scr[rows_sl, :] = _conv_silu(sbc[:, N:], c_ref[rows_sl, :].astype(F32), cwc_ref, cbc_ref).astype(BF16)

        dt = _softplus(dt_ref[c] + dtb_ref[...])
        a = neg_a * dt
        a3 = jnp.concatenate(_split3(a), axis=0)
        pre = jnp.dot(a3, tri_pre, preferred_element_type=F32)
        suf = jnp.dot(a3, tri_suf, preferred_element_type=F32)
        pre = pre[0:2 * H] + pre[2 * H:4 * H] + pre[4 * H:6 * H]
        suf = suf[0:2 * H] + suf[2 * H:4 * H] + suf[4 * H:6 * H]
        is_fwd = lax.broadcasted_iota(jnp.int32, (2 * H, L), 0) < H
        rows = jnp.concatenate([jnp.where(is_fwd, pre, suf), dt], axis=0)
        rows_scr[c] = rows
        padded = jnp.concatenate([rows, jnp.zeros((L - 4 * H, L), F32)], axis=0)
        cols_scr[rows_sl, :] = padded.T
        return 0

    lax.fori_loop(0, nchunks, prep, 0, unroll=8)

    row_h = lax.broadcasted_iota(jnp.int32, (H * L, SSM_GW), 0) // L
    lane_h = lax.broadcasted_iota(jnp.int32, (H * L, SSM_GW), 1) // SSM_HEAD_DIM

    def chunk(c, d, h_scr):
        rows_sl = pl.ds(pl.multiple_of(c * L, L), L)
        rows = rows_scr[c]
        cols = cols_scr[rows_sl, :]
        xc = x_scr[rows_sl, :]
        cm = c_scr[rows_sl, :]
        bt = bt_scr[c]
        scores = jnp.dot(cm, bt, preferred_element_type=F32)
        scores = jnp.where((lane_i <= row_i) if d == 0 else (lane_i >= row_i), scores, 0.0)
        cum_bc = [col_bcast(cols, d * H + h) for h in range(H)]
        dt_e = jnp.concatenate(
            [jnp.take_along_axis(cols, 2 * H + d * H + 2 * half + lane128 // SSM_HEAD_DIM, axis=1)
             for half in range(2)], axis=1)
        seg = (jnp.concatenate(cum_bc, axis=1)
               - jnp.concatenate([rows[d * H + h:d * H + h + 1, :] for h in range(H)], axis=1))
        decay = jnp.exp2(jnp.minimum(seg, 0.0))
        mix = jnp.concatenate([scores.astype(BF16)] * H, axis=1) * decay.astype(BF16)
        cum_e = expand(cum_bc)
        xdt = xc * dt_e
        xdt_b = xdt.astype(BF16)
        x_bd = jnp.where(row_h == lane_h, jnp.concatenate([xdt_b] * H, axis=0), jnp.zeros((), BF16))
        y = jnp.dot(mix, x_bd, preferred_element_type=F32)
        state = h_scr[...]
        y = y + jnp.dot(cm, state.astype(BF16), preferred_element_type=F32) * jnp.exp2(cum_e)
        tot = cum_e[L - 1:L, :] if d == 0 else cum_e[0:1, :]
        xe = (xdt * jnp.exp2(tot - cum_e)).astype(BF16)
        h_scr[...] = jnp.exp2(tot) * state + jnp.dot(bt, xe, preferred_element_type=F32)
        return y

    hf_scr[...] = jnp.zeros_like(hf_scr)
    hb_scr[...] = jnp.zeros_like(hb_scr)

    def scan(i, _):
        j = nchunks - 1 - i
        yf_scr[pl.ds(pl.multiple_of(i * L, L), L), :] = chunk(i, 0, hf_scr)
        yb_scr[pl.ds(pl.multiple_of(j * L, L), L), :] = chunk(j, 1, hb_scr)
        return 0

    lax.fori_loop(0, nchunks, scan, 0, unroll=8)

    def finish(c, _):
        rows_sl = pl.ds(pl.multiple_of(c * L, L), L)
        z = z_ref[rows_sl, :].astype(F32)
        y = yf_scr[rows_sl, :] + yb_scr[rows_sl, :] + dsk_ref[...] * x_scr[rows_sl, :]
        y = y * (z * _sigmoid(z))
        ms = jnp.mean(y * y, axis=-1, keepdims=True)
        y_ref[rows_sl, :] = (y * lax.rsqrt(ms + EPS) * ng_ref[...]).astype(y_ref.dtype)
        return 0

    lax.fori_loop(0, nchunks, finish, 0, unroll=2)


def _ssd(proj3, dt5, conv_w, conv_b, dtb, alog, dsk, ng):
    bsz, seq, _ = proj3.shape
    L, G, GW, N = SSM_CHUNK, SSM_GROUPS, SSM_GW, SSM_STATE
    nchunks = seq // L
    k = SSM_CONV_K
    col = lambda base, w: (lambda b, g: (b, 0, base // w + g))
    pcol = lambda base, w: (lambda b, g: (0, base // w + g))
    return pl.pallas_call(
        _ssd_kernel,
        grid=(bsz, G),
        in_specs=[
            pl.BlockSpec((None, seq, GW), col(COL_Z, GW)),
            pl.BlockSpec((None, seq, GW), col(COL_XS, GW)),
            pl.BlockSpec((None, seq, N), col(COL_B, N)),
            pl.BlockSpec((None, seq, N), col(COL_C, N)),
            pl.BlockSpec((None, None, nchunks, 2 * SSM_HPG, L), lambda b, g: (b, g, 0, 0, 0)),
            pl.BlockSpec((k, GW), pcol(0, GW)),
            pl.BlockSpec((k, N), pcol(SSM_D_INNER, N)),
            pl.BlockSpec((k, N), pcol(SSM_D_INNER + G * N, N)),
            pl.BlockSpec((1, GW), pcol(0, GW)),
            pl.BlockSpec((1, N), pcol(SSM_D_INNER, N)),
            pl.BlockSpec((1, N), pcol(SSM_D_INNER + G * N, N)),
            pl.BlockSpec((None, 2 * SSM_HPG, 1), lambda b, g: (g, 0, 0)),
            pl.BlockSpec((None, 2 * SSM_HPG, 1), lambda b, g: (g, 0, 0)),
            pl.BlockSpec((1, GW), pcol(0, GW)),
            pl.BlockSpec((1, GW), pcol(0, GW)),
        ],
        out_specs=pl.BlockSpec((None, seq, GW), lambda b, g: (b, 0, g)),
        out_shape=jax.ShapeDtypeStruct((bsz, seq, SSM_D_INNER), BF16),
        scratch_shapes=[
            pltpu.VMEM((3, len(_CONV_SIDE_TAPS) * L, CONV_WIN), BF16),
            pltpu.VMEM((seq, GW), F32),
            pltpu.VMEM((nchunks, N, L), BF16),
            pltpu.VMEM((seq, N), BF16),
            pltpu.VMEM((nchunks, 4 * SSM_HPG, L), F32),
            pltpu.VMEM((seq, L), F32),
            pltpu.VMEM((seq, GW), F32),
            pltpu.VMEM((seq, GW), F32),
            pltpu.VMEM((N, GW), F32),
            pltpu.VMEM((N, GW), F32),
        ],
        compiler_params=_params("parallel", "parallel"),
        name="ssd",
    )(proj3, proj3, proj3, proj3, dt5, conv_w, conv_w, conv_w, conv_b, conv_b, conv_b, dtb, alog, dsk, ng)


def _rms_rope(x, g, cos, sin):
    ms = jnp.mean(x * x, axis=-1, keepdims=True)
    x = x * lax.rsqrt(ms + EPS) * g
    return x * cos + pltpu.roll(x, ATTN_HEAD_DIM // 2, axis=1) * sin


def _attn_kernel(qc_ref, qn_ref, k_ref, v_ref, cos_ref, sin_ref, gq_ref, gk_ref, o_ref,
                       q_scr, k_scr, vx_scr, s_scr, *, tq, tk, rows):
    seq = k_ref.shape[0]
    hd = ATTN_HEAD_DIM
    nrow = ATTN_REP * tq
    qi = pl.program_id(2)
    nq = pl.num_programs(2)
    scale = hd ** -0.5 * LOG2E

    def prep_q(q_ref, tile, slot):
        sl = pl.ds(pl.multiple_of(tile * tq, tq), tq)
        cos = cos_ref[sl, :]
        sin = sin_ref[sl, :]
        for r in range(ATTN_REP):
            q = _rms_rope(q_ref[:, r * hd:(r + 1) * hd].astype(F32), gq_ref[...], cos, sin)
            q_scr[slot, r * tq:(r + 1) * tq, :] = (q * scale).astype(BF16)

    @pl.when(qi == 0)
    def _():
        def body(r, _):
            sl = pl.ds(pl.multiple_of(r * rows, rows), rows)
            k = _rms_rope(k_ref[sl, :].astype(F32), gk_ref[...], cos_ref[sl, :], sin_ref[sl, :])
            k_scr[sl, :] = k.astype(BF16)
            return 0

        lax.fori_loop(0, seq // rows, body, 0)
        vx_scr[:, :hd] = v_ref[...]
        vx_scr[:, hd:] = jnp.ones((seq, hd), BF16)
        prep_q(qc_ref, 0, 0)

    q_all = q_scr[qi % 2]

    m_acc = jnp.full((nrow, hd), -jnp.inf, F32)
    for kb in range(seq // tk):
        s = lax.dot_general(q_all, k_scr[kb * tk:(kb + 1) * tk, :], (((1,), (1,)), ((), ())),
                            preferred_element_type=F32)
        s_scr[:, kb * tk:(kb + 1) * tk] = s
        for lt in range(tk // hd):
            m_acc = jnp.maximum(m_acc, s[:, lt * hd:(lt + 1) * hd])
    m = jnp.max(m_acc, axis=-1, keepdims=True)

    prep_q(qn_ref, jnp.minimum(qi + 1, nq - 1), (qi + 1) % 2)

    half = nrow // 2
    acc_lo = jnp.zeros((half, 2 * hd), F32)
    acc_hi = jnp.zeros((half, 2 * hd), F32)
    for kb in range(seq // tk):
        pb = jnp.exp2(s_scr[:, kb * tk:(kb + 1) * tk] - m).astype(BF16)
        vb = vx_scr[kb * tk:(kb + 1) * tk, :]
        acc_lo = acc_lo + jnp.dot(pb[:half], vb, preferred_element_type=F32)
        acc_hi = acc_hi + jnp.dot(pb[half:], vb, preferred_element_type=F32)
    acc = jnp.concatenate([acc_lo, acc_hi], axis=0)
    out = acc[:, :hd] / acc[:, hd:]
    for r in range(ATTN_REP):
        o_ref[:, r * hd:(r + 1) * hd] = out[r * tq:(r + 1) * tq].astype(o_ref.dtype)


def _attn(proj3, cos, sin, gq, gk, *, tq, tk):
    bsz, seq, _ = proj3.shape
    hd = ATTN_HEAD_DIM
    qw = ATTN_REP * hd
    nq = seq // tq
    return pl.pallas_call(
        functools.partial(_attn_kernel, tq=tq, tk=tk, rows=min(seq, 512)),
        grid=(bsz, ATTN_KV_HEADS, nq),
        in_specs=[
            pl.BlockSpec((None, tq, qw), lambda b, h, i: (b, i, COL_Q // qw + h)),
            pl.BlockSpec((None, tq, qw), lambda b, h, i: (b, jnp.minimum(i + 1, nq - 1), COL_Q // qw + h)),
            pl.BlockSpec((None, seq, hd), lambda b, h, i: (b, 0, COL_K // hd + h)),
            pl.BlockSpec((None, seq, hd), lambda b, h, i: (b, 0, COL_V // hd + h)),
            pl.BlockSpec((seq, hd), lambda b, h, i: (0, 0)),
            pl.BlockSpec((seq, hd), lambda b, h, i: (0, 0)),
            pl.BlockSpec((1, hd), lambda b, h, i: (0, 0)),
            pl.BlockSpec((1, hd), lambda b, h, i: (0, 0)),
        ],
        out_specs=pl.BlockSpec((None, tq, qw), lambda b, h, i: (b, i, h)),
        out_shape=jax.ShapeDtypeStruct((bsz, seq, ATTN_Q_HEADS * hd), BF16),
        scratch_shapes=[
            pltpu.VMEM((2, ATTN_REP * tq, hd), BF16),
            pltpu.VMEM((seq, hd), BF16),
            pltpu.VMEM((seq, 2 * hd), BF16),
            pltpu.VMEM((ATTN_REP * tq, seq), F32),
        ],
        compiler_params=_params("parallel", "parallel", "arbitrary"),
        name="attn",
    )(proj3, proj3, proj3, proj3, cos, sin, gq, gk)


def _merge_kernel(ys_ref, ya_ref, wa_ref, wb_ref, ga_ref, gb_ref, o_ref):
    a = jnp.dot(ys_ref[...], wa_ref[...], preferred_element_type=F32)
    b = jnp.dot(ya_ref[...], wb_ref[...], preferred_element_type=F32)
    merged = _sigmoid(ga_ref[...].astype(F32)) * a + _sigmoid(gb_ref[...].astype(F32)) * b
    o_ref[...] = merged.astype(o_ref.dtype)


def _merge(ys, ya, wa, wb, proj, *, tm, tn):
    m, d = ys.shape
    return pl.pallas_call(
        _merge_kernel,
        grid=(m // tm, D_MODEL // tn),
        in_specs=[
            pl.BlockSpec((tm, d), lambda i, j: (i, 0)),
            pl.BlockSpec((tm, d), lambda i, j: (i, 0)),
            pl.BlockSpec((d, tn), lambda i, j: (0, j)),
            pl.BlockSpec((d, tn), lambda i, j: (0, j)),
            pl.BlockSpec((tm, tn), lambda i, j: (i, COL_GA // tn + j)),
            pl.BlockSpec((tm, tn), lambda i, j: (i, COL_GB // tn + j)),
        ],
        out_specs=pl.BlockSpec((tm, tn), lambda i, j: (i, j)),
        out_shape=jax.ShapeDtypeStruct((m, D_MODEL), BF16),
        compiler_params=_params("parallel", "arbitrary"),
        name="merge",
    )(ys, ya, wa, wb, proj, proj)


def _outp_kernel(mg_ref, wo_ref, x_ref, g_ref, x1_ref, h2_ref):
    x1 = x_ref[...] + jnp.dot(mg_ref[...], wo_ref[...], preferred_element_type=F32)
    x1_ref[...] = x1
    ms = jnp.mean(x1 * x1, axis=-1, keepdims=True)
    h2_ref[...] = (x1 * lax.rsqrt(ms + EPS) * g_ref[...]).astype(h2_ref.dtype)


def _outp(merged, wo, x2d, g, *, tm):
    m, d = x2d.shape
    return pl.pallas_call(
        _outp_kernel,
        grid=(m // tm,),
        in_specs=[
            pl.BlockSpec((tm, d), lambda i: (i, 0)),
            pl.BlockSpec((d, d), lambda i: (0, 0)),
            pl.BlockSpec((tm, d), lambda i: (i, 0)),
            pl.BlockSpec((1, d), lambda i: (0, 0)),
        ],
        out_specs=[
            pl.BlockSpec((tm, d), lambda i: (i, 0)),
            pl.BlockSpec((tm, d), lambda i: (i, 0)),
        ],
        out_shape=[
            jax.ShapeDtypeStruct((m, d), F32),
            jax.ShapeDtypeStruct((m, d), BF16),
        ],
        compiler_params=_params("parallel"),
        name="outp",
    )(merged, wo, x2d, g)


def _mlp_kernel(h2_ref, w1_ref, w2_ref, x1_ref, g_ref, o_ref, *, final_norm):
    f = pl.program_id(1)

    @pl.when(f == 0)
    def _():
        o_ref[...] = x1_ref[...]

    act = jnp.maximum(jnp.dot(h2_ref[...], w1_ref[...], preferred_element_type=F32), 0.0)
    o_ref[...] += jnp.dot((act * act).astype(BF16), w2_ref[...], preferred_element_type=F32)

    if final_norm:
        @pl.when(f == pl.num_programs(1) - 1)
        def _():
            y = o_ref[...]
            ms = jnp.mean(y * y, axis=-1, keepdims=True)
            o_ref[...] = y * lax.rsqrt(ms + EPS) * g_ref[...]


def _mlp(h2, w1, w2, x1, g, *, tm, tf, final_norm):
    m, d = x1.shape
    ff = w1.shape[1]
    return pl.pallas_call(
        functools.partial(_mlp_kernel, final_norm=final_norm),
        grid=(m // tm, ff // tf),
        in_specs=[
            pl.BlockSpec((tm, d), lambda i, f: (i, 0)),
            pl.BlockSpec((d, tf), lambda i, f: (0, f)),
            pl.BlockSpec((tf, d), lambda i, f: (f, 0)),
            pl.BlockSpec((tm, d), lambda i, f: (i, 0)),
            pl.BlockSpec((1, d), lambda i, f: (0, 0)),
        ],
        out_specs=pl.BlockSpec((tm, d), lambda i, f: (i, 0)),
        out_shape=jax.ShapeDtypeStruct((m, d), F32),
        compiler_params=_params("parallel", "arbitrary"),
        name="mlp",
    )(h2, w1, w2, x1, g)


def _rope_tables(seq):
    rows = seq // GRID_W
    row = jnp.repeat(jnp.arange(rows, dtype=F32), GRID_W)
    col = jnp.tile(jnp.arange(GRID_W, dtype=F32), rows)
    half = ATTN_HEAD_DIM // 2
    inv_freq = 1.0 / (ROPE_THETA ** (jnp.arange(0, half, 2, dtype=F32) / half))
    ang = jnp.concatenate([row[:, None] * inv_freq, col[:, None] * inv_freq], axis=-1)
    cos = jnp.concatenate([jnp.cos(ang), jnp.cos(ang)], axis=-1)
    sin = jnp.concatenate([-jnp.sin(ang), jnp.sin(ang)], axis=-1)
    return cos, sin


def _deinterleave_heads(a, nheads):
    lead = a.shape[:-1]
    a = a.reshape(lead + (nheads, ATTN_HEAD_DIM // 2, 2))
    return jnp.swapaxes(a, -1, -2).reshape(lead + (nheads * ATTN_HEAD_DIM,))


def _group_rows(v_f, v_b):
    g = jnp.concatenate([v_f.reshape(SSM_GROUPS, SSM_HPG), v_b.reshape(SSM_GROUPS, SSM_HPG)], axis=1)
    return g[:, :, None].astype(F32)


def kernel(x, g_mix, w_in, conv_w, conv_b, dt_bias_f, dt_bias_b, a_log_f, a_log_b, d_skip, ssm_norm_g,
           q_norm_g, k_norm_g, w_ssm_up, w_attn_up, w_out, g_mlp, w_mlp_in, w_mlp_out, g_final):
    bsz, seq, d = x.shape
    m = bsz * seq
    depth = w_in.shape[0]
    cos, sin = _rope_tables(seq)
    dt_lo = SSM_D_INNER + SSM_XBC
    dt_hi = dt_lo + 2 * SSM_HEADS
    nchunks = seq // SSM_CHUNK
    tm_big = min(m, 1024)
    tm_mid = min(m, 512)

    x2d = x.reshape(m, d)
    for layer in range(depth):
        w = w_in[layer]
        q_lo = dt_hi
        k_lo = q_lo + ATTN_Q_HEADS * ATTN_HEAD_DIM
        v_lo = k_lo + ATTN_KV_HEADS * ATTN_HEAD_DIM
        w_all = jnp.concatenate([
            w[:, :dt_lo].astype(BF16),
            _deinterleave_heads(w[:, q_lo:k_lo], ATTN_Q_HEADS).astype(BF16),
            _deinterleave_heads(w[:, k_lo:v_lo], ATTN_KV_HEADS).astype(BF16),
            w[:, v_lo:].astype(BF16),
            w[:, dt_lo:dt_hi].astype(BF16),
            jnp.zeros((d, DT_PAD - 2 * SSM_HEADS), BF16)], axis=1)
        proj, dt = _proj(x2d, g_mix[layer][None, :], w_all, tm=tm_big, tn=1024)
        proj3 = proj.reshape(bsz, seq, N_MAIN)

        dt5 = dt[:, :2 * SSM_HEADS].reshape(bsz, nchunks, SSM_CHUNK, 2, SSM_GROUPS, SSM_HPG)
        dt5 = dt5.transpose(0, 4, 1, 3, 5, 2).reshape(bsz, SSM_GROUPS, nchunks, 2 * SSM_HPG, SSM_CHUNK)
        y_ssm = _ssd(
            proj3, dt5, conv_w[layer], conv_b[layer][None, :],
            _group_rows(dt_bias_f[layer], dt_bias_b[layer]),
            _group_rows(a_log_f[layer], a_log_b[layer]),
            jnp.repeat(d_skip[layer].astype(F32), SSM_HEAD_DIM)[None, :],
            ssm_norm_g[layer][None, :])

        y_attn = _attn(proj3, cos, sin, _deinterleave_heads(q_norm_g[layer], 1)[None, :],
                       _deinterleave_heads(k_norm_g[layer], 1)[None, :],
                       tq=min(seq, 256), tk=min(seq, 512))

        merged = _merge(y_ssm.reshape(m, SSM_D_INNER), y_attn.reshape(m, ATTN_Q_HEADS * ATTN_HEAD_DIM),
                        w_ssm_up[layer].astype(BF16), w_attn_up[layer].astype(BF16), proj,
                        tm=tm_big, tn=512)
        x1, h2 = _outp(merged, w_out[layer].astype(BF16), x2d, g_mlp[layer][None, :], tm=tm_mid)
        x2d = _mlp(h2, w_mlp_in[layer].astype(BF16), w_mlp_out[layer].astype(BF16), x1, g_final[None, :],
                   tm=tm_big, tf=512, final_norm=layer == depth - 1)
    return x2d.reshape(bsz, seq, d)
```

```python
import functools

import jax
import jax.numpy as jnp
import numpy as np
from jax import lax
from jax.experimental import pallas as pl
from jax.experimental.pallas import tpu as pltpu

F32 = jnp.float32
BF16 = jnp.bfloat16

EPS = 1e-6
LOG2E = 1.4426950408889634
D_MODEL = 2048
SSM_D_INNER = 2048
SSM_HEAD_DIM = 64
SSM_HEADS = SSM_D_INNER // SSM_HEAD_DIM
SSM_GROUPS = 8
SSM_HPG = SSM_HEADS // SSM_GROUPS
SSM_GW = SSM_D_INNER // SSM_GROUPS
SSM_STATE = 128
SSM_CONV_K = 5
SSM_CHUNK = 128
SSM_XBC = SSM_D_INNER + 2 * SSM_GROUPS * SSM_STATE
ATTN_HEAD_DIM = 128
ATTN_Q_HEADS = 16
ATTN_KV_HEADS = 4
ATTN_REP = ATTN_Q_HEADS // ATTN_KV_HEADS
ROPE_THETA = 10000.0
GRID_W = 64
D_FF = 4 * D_MODEL

COL_Z = 0
COL_XS = COL_Z + SSM_D_INNER
COL_B = COL_XS + SSM_D_INNER
COL_C = COL_B + SSM_GROUPS * SSM_STATE
COL_Q = COL_C + SSM_GROUPS * SSM_STATE
COL_K = COL_Q + ATTN_Q_HEADS * ATTN_HEAD_DIM
COL_V = COL_K + ATTN_KV_HEADS * ATTN_HEAD_DIM
COL_GA = COL_V + ATTN_KV_HEADS * ATTN_HEAD_DIM
COL_GB = COL_GA + D_MODEL
N_MAIN = COL_GB + D_MODEL
DT_PAD = 128

V7X_VMEM_LIMIT = 56 * 1024 * 1024
CONV_HALO = 16
CONV_WIN = SSM_CHUNK + 2 * CONV_HALO


def _params(*sem):
    return pltpu.CompilerParams(dimension_semantics=sem, vmem_limit_bytes=V7X_VMEM_LIMIT)


def _sigmoid(x):
    return 1.0 / (1.0 + jnp.exp(-x))


def _softplus(x):
    return jnp.maximum(x, 0.0) + jnp.log(1.0 + jnp.exp(-jnp.abs(x)))


_NT_DIMS = (((1,), (1,)), ((), ()))


def _proj_kernel(x_ref, g_ref, wa_ref, wb_ref, wdt_ref, o_ref, dt_ref, h_scr, *, rows, na):
    j = pl.program_id(1)

    @pl.when(j == 0)
    def _():
        def body(r, _):
            sl = pl.ds(pl.multiple_of(r * rows, rows), rows)
            x = x_ref[sl, :]
            ms = jnp.mean(x * x, axis=-1, keepdims=True)
            hb = (x * lax.rsqrt(ms + EPS) * g_ref[...]).astype(BF16)
            h_scr[sl, :] = hb
            dt_ref[sl, :] = lax.dot_general(hb, wdt_ref[...], _NT_DIMS, preferred_element_type=F32)
            return 0

        lax.fori_loop(0, x_ref.shape[0] // rows, body, 0)

    @pl.when(j < na)
    def _():
        o_ref[...] = lax.dot_general(h_scr[...], wa_ref[...].astype(BF16), _NT_DIMS,
                                     preferred_element_type=F32).astype(o_ref.dtype)

    @pl.when(j >= na)
    def _():
        o_ref[...] = lax.dot_general(h_scr[...], wb_ref[...], _NT_DIMS,
                                     preferred_element_type=F32).astype(o_ref.dtype)


def _proj(x2d, g, wa_t, wb_t, *, tm, tn):
    m, d = x2d.shape
    n = N_MAIN
    na = COL_Q // tn
    return pl.pallas_call(
        functools.partial(_proj_kernel, rows=min(tm, 256), na=na),
        grid=(m // tm, n // tn),
        in_specs=[
            pl.BlockSpec((tm, d), lambda i, j: (i, 0)),
            pl.BlockSpec((1, d), lambda i, j: (0, 0)),
            pl.BlockSpec((tn, d), lambda i, j: (jnp.minimum(j, na - 1), 0)),
            pl.BlockSpec((tn, d), lambda i, j: (jnp.maximum(j - na, 0), 0)),
            pl.BlockSpec((DT_PAD, d), lambda i, j: ((N_MAIN - COL_Q) // DT_PAD, 0)),
        ],
        out_specs=[
            pl.BlockSpec((tm, tn), lambda i, j: (i, j)),
            pl.BlockSpec((tm, DT_PAD), lambda i, j: (i, 0)),
        ],
        out_shape=[
            jax.ShapeDtypeStruct((m, n), BF16),
            jax.ShapeDtypeStruct((m, DT_PAD), F32),
        ],
        scratch_shapes=[pltpu.VMEM((tm, d), BF16)],
        compiler_params=_params("parallel", "arbitrary"),
        name="proj",
    )(x2d, g, wa_t, wb_t, wb_t)


_CONV_SIDE_TAPS = tuple(k for k in range(SSM_CONV_K) if k != SSM_CONV_K // 2)


def _conv_window(c, seq):
    L = SSM_CHUNK
    start = jnp.clip(c * L - CONV_HALO, 0, seq - CONV_WIN)
    start = pl.multiple_of(start, CONV_HALO)
    return start, (c * L - start) // CONV_HALO


def _shift_matrix(halo_units):
    L = SSM_CHUNK
    t_idx = lax.broadcasted_iota(jnp.int32, (L, CONV_WIN), 0)
    j_idx = lax.broadcasted_iota(jnp.int32, (L, CONV_WIN), 1)
    base = j_idx - t_idx - halo_units * CONV_HALO
    shifts = [jnp.where(base == k - SSM_CONV_K // 2, 1.0, 0.0).astype(BF16) for k in _CONV_SIDE_TAPS]
    return jnp.concatenate(shifts, axis=0)


def _conv_silu(shifted, centre, w_ref, b_ref):
    L = SSM_CHUNK
    mid = SSM_CONV_K // 2
    acc = centre * w_ref[mid:mid + 1, :] + b_ref[...]
    for n, k in enumerate(_CONV_SIDE_TAPS):
        acc = acc + shifted[n * L:(n + 1) * L] * w_ref[k:k + 1, :]
    return acc * _sigmoid(acc)


def _split3(a):
    a1 = a.astype(BF16)
    r1 = a - a1.astype(F32)
    a2 = r1.astype(BF16)
    a3 = (r1 - a2.astype(F32)).astype(BF16)
    return a1, a2, a3


def _ssd_kernel(z_ref, xs_ref, b_ref, c_ref, dt_ref, cwx_ref, cwb_ref, cwc_ref, cbx_ref, cbb_ref,
                cbc_ref, dtb_ref, alog_ref, dsk_ref, ng_ref, y_ref,
                shift_scr, x_scr, bt_scr, c_scr, rows_scr, cols_scr, yf_scr, yb_scr, hf_scr, hb_scr):
    L = SSM_CHUNK
    H = SSM_HPG
    N = SSM_STATE
    seq = xs_ref.shape[0]
    nchunks = seq // L

    row_i = lax.broadcasted_iota(jnp.int32, (L, L), 0)
    lane_i = lax.broadcasted_iota(jnp.int32, (L, L), 1)
    tri_pre = jnp.where(row_i <= lane_i, 1.0, 0.0).astype(BF16)
    tri_suf = jnp.where(row_i >= lane_i, 1.0, 0.0).astype(BF16)
    neg_a = -jnp.exp(alog_ref[...]) * LOG2E
    lane128 = lax.broadcasted_iota(jnp.int32, (L, L), 1)

    def col_bcast(cols, lane):
        return jnp.broadcast_to(cols[:, lane:lane + 1], (L, L))

    def expand(bc):
        lo = jnp.where(lane128 < SSM_HEAD_DIM, bc[0], bc[1])
        hi = jnp.where(lane128 < SSM_HEAD_DIM, bc[2], bc[3])
        return jnp.concatenate([lo, hi], axis=1)

    for halo_units in range(3):
        shift_scr[halo_units] = _shift_matrix(halo_units)

    def prep(c, _):
        rows_sl = pl.ds(pl.multiple_of(c * L, L), L)
        start, halo_units = _conv_window(c, seq)
        shift = shift_scr[halo_units]
        win = pl.ds(start, CONV_WIN)
        sx = jnp.dot(shift, xs_ref[win, :], preferred_element_type=F32)
        sbc = jnp.dot(shift, jnp.concatenate([b_ref[win, :], c_ref[win, :]], axis=1),
                      preferred_element_type=F32)
        xc = _conv_silu(sx, xs_ref[rows_sl, :].astype(F32), cwx_ref, cbx_ref)
        x_scr[rows_sl, :] = xc
        bm = _conv_silu(sbc[:, :N], b_ref[rows_sl, :].astype(F32), cwb_ref, cbb_ref)
        bt_scr[c] = bm.T.astype(BF16)
        c_scr[rows_sl, :] = _conv_silu(sbc[:, N:], c_ref[rows_sl, :].astype(F32), cwc_ref, cbc_ref).astype(BF16)

        dt = _softplus(dt_ref[c] + dtb_ref[...])
        a = neg_a * dt
        a3 = jnp.concatenate(_split3(a), axis=0)
        pre = jnp.dot(a3, tri_pre, preferred_element_type=F32)
        suf = jnp.dot(a3, tri_suf, preferred_element_type=F32)
        pre = pre[0:2 * H] + pre[2 * H:4 * H] + pre[4 * H:6 * H]
        suf = suf[0:2 * H] + suf[2 * H:4 * H] + suf[4 * H:6 * H]
        is_fwd = lax.broadcasted_iota(jnp.int32, (2 * H, L), 0) < H
        rows = jnp.concatenate([jnp.where(is_fwd, pre, suf), dt], axis=0)
        rows_scr[c] = rows
        padded = jnp.concatenate([rows, jnp.zeros((L - 4 * H, L), F32)], axis=0)
        cols_scr[rows_sl, :] = padded.T
        return 0

    lax.fori_loop(0, nchunks, prep, 0, unroll=8)

    row_h = lax.broadcasted_iota(jnp.int32, (H * L, SSM_GW), 0) // L
    lane_h = lax.broadcasted_iota(jnp.int32, (H * L, SSM_GW), 1) // SSM_HEAD_DIM

    def chunk(c, d, h_scr):
        rows_sl = pl.ds(pl.multiple_of(c * L, L), L)
        rows = rows_scr[c]
        cols = cols_scr[rows_sl, :]
        xc = x_scr[rows_sl, :]
        cm = c_scr[rows_sl, :]
        bt = bt_scr[c]
        scores = jnp.dot(cm, bt, preferred_element_type=F32)
        scores = jnp.where((lane_i <= row_i) if d == 0 else (lane_i >= row_i), scores, 0.0)
        cum_bc = [col_bcast(cols, d * H + h) for h in range(H)]
        dt_e = jnp.concatenate(
            [jnp.take_along_axis(cols, 2 * H + d * H + 2 * half + lane128 // SSM_HEAD_DIM, axis=1)
             for half in range(2)], axis=1)
        seg = (jnp.concatenate(cum_bc, axis=1)
               - jnp.concatenate([rows[d * H + h:d * H + h + 1, :] for h in range(H)], axis=1))
        decay = jnp.exp2(jnp.minimum(seg, 0.0))
        mix = jnp.concatenate([scores.astype(BF16)] * H, axis=1) * decay.astype(BF16)
        cum_e = expand(cum_bc)
        xdt = xc * dt_e
        xdt_b = xdt.astype(BF16)
        x_bd = jnp.where(row_h == lane_h, jnp.concatenate([xdt_b] * H, axis=0), jnp.zeros((), BF16))
        y = jnp.dot(mix, x_bd, preferred_element_type=F32)
        state = h_scr[...]
        y = y + jnp.dot(cm, state.astype(BF16), preferred_element_type=F32) * jnp.exp2(cum_e)
        tot = cum_e[L - 1:L, :] if d == 0 else cum_e[0:1, :]
        xe = (xdt * jnp.exp2(tot - cum_e)).astype(BF16)
        h_scr[...] = jnp.exp2(tot) * state + jnp.dot(bt, xe, preferred_element_type=F32)
        return y

    hf_scr[...] = jnp.zeros_like(hf_scr)
    hb_scr[...] = jnp.zeros_like(hb_scr)

    def scan(i, _):
        j = nchunks - 1 - i
        yf_scr[pl.ds(pl.multiple_of(i * L, L), L), :] = chunk(i, 0, hf_scr)
        yb_scr[pl.ds(pl.multiple_of(j * L, L), L), :] = chunk(j, 1, hb_scr)
        return 0

    lax.fori_loop(0, nchunks, scan, 0, unroll=8)

    def finish(c, _):
        rows_sl = pl.ds(pl.multiple_of(c * L, L), L)
        z = z_ref[rows_sl, :].astype(F32)
        y = yf_scr[rows_sl, :] + yb_scr[rows_sl, :] + dsk_ref[...] * x_scr[rows_sl, :]
        y = y * (z * _sigmoid(z))
        ms = jnp.mean(y * y, axis=-1, keepdims=True)
        y_ref[rows_sl, :] = (y * lax.rsqrt(ms + EPS) * ng_ref[...]).astype(y_ref.dtype)
        return 0

    lax.fori_loop(0, nchunks, finish, 0, unroll=2)


def _ssd(proj3, dt5, conv_w, conv_b, dtb, alog, dsk, ng):
    bsz, seq, _ = proj3.shape
    L, G, GW, N = SSM_CHUNK, SSM_GROUPS, SSM_GW, SSM_STATE
    nchunks = seq // L
    k = SSM_CONV_K
    col = lambda base, w: (lambda b, g: (b, 0, base // w + g))
    pcol = lambda base, w: (lambda b, g: (0, base // w + g))
    return pl.pallas_call(
        _ssd_kernel,
        grid=(bsz, G),
        in_specs=[
            pl.BlockSpec((None, seq, GW), col(COL_Z, GW)),
            pl.BlockSpec((None, seq, GW), col(COL_XS, GW)),
            pl.BlockSpec((None, seq, N), col(COL_B, N)),
            pl.BlockSpec((None, seq, N), col(COL_C, N)),
            pl.BlockSpec((None, None, nchunks, 2 * SSM_HPG, L), lambda b, g: (b, g, 0, 0, 0)),
            pl.BlockSpec((k, GW), pcol(0, GW)),
            pl.BlockSpec((k, N), pcol(SSM_D_INNER, N)),
            pl.BlockSpec((k, N), pcol(SSM_D_INNER + G * N, N)),
            pl.BlockSpec((1, GW), pcol(0, GW)),
            pl.BlockSpec((1, N), pcol(SSM_D_INNER, N)),
            pl.BlockSpec((1, N), pcol(SSM_D_INNER + G * N, N)),
            pl.BlockSpec((None, 2 * SSM_HPG, 1), lambda b, g: (g, 0, 0)),
            pl.BlockSpec((None, 2 * SSM_HPG, 1), lambda b, g: (g, 0, 0)),
            pl.BlockSpec((1, GW), pcol(0, GW)),
            pl.BlockSpec((1, GW), pcol(0, GW)),
        ],
        out_specs=pl.BlockSpec((None, seq, GW), lambda b, g: (b, 0, g)),
        out_shape=jax.ShapeDtypeStruct((bsz, seq, SSM_D_INNER), BF16),
        scratch_shapes=[
            pltpu.VMEM((3, len(_CONV_SIDE_TAPS) * L, CONV_WIN), BF16),
            pltpu.VMEM((seq, GW), F32),
            pltpu.VMEM((nchunks, N, L), BF16),
            pltpu.VMEM((seq, N), BF16),
            pltpu.VMEM((nchunks, 4 * SSM_HPG, L), F32),
            pltpu.VMEM((seq, L), F32),
            pltpu.VMEM((seq, GW), F32),
            pltpu.VMEM((seq, GW), F32),
            pltpu.VMEM((N, GW), F32),
            pltpu.VMEM((N, GW), F32),
        ],
        compiler_params=_params("parallel", "parallel"),
        name="ssd",
    )(proj3, proj3, proj3, proj3, dt5, conv_w, conv_w, conv_w, conv_b, conv_b, conv_b, dtb, alog, dsk, ng)


def _rms_rope(x, g, cos, sin):
    ms = jnp.mean(x * x, axis=-1, keepdims=True)
    x = x * lax.rsqrt(ms + EPS) * g
    return x * cos + pltpu.roll(x, ATTN_HEAD_DIM // 2, axis=1) * sin


def _attn_kernel(qc_ref, qn_ref, k_ref, v_ref, cos_ref, sin_ref, gq_ref, gk_ref, o_ref,
                       q_scr, k_scr, vx_scr, s_scr, *, tq, tk, rows):
    seq = k_ref.shape[0]
    hd = ATTN_HEAD_DIM
    nrow = ATTN_REP * tq
    qi = pl.program_id(2)
    nq = pl.num_programs(2)
    scale = hd ** -0.5 * LOG2E

    def prep_q(q_ref, tile, slot):
        sl = pl.ds(pl.multiple_of(tile * tq, tq), tq)
        cos = cos_ref[sl, :]
        sin = sin_ref[sl, :]
        for r in range(ATTN_REP):
            q = _rms_rope(q_ref[:, r * hd:(r + 1) * hd].astype(F32), gq_ref[...], cos, sin)
            q_scr[slot, r * tq:(r + 1) * tq, :] = (q * scale).astype(BF16)

    @pl.when(qi == 0)
    def _():
        def body(r, _):
            sl = pl.ds(pl.multiple_of(r * rows, rows), rows)
            k = _rms_rope(k_ref[sl, :].astype(F32), gk_ref[...], cos_ref[sl, :], sin_ref[sl, :])
            k_scr[sl, :] = k.astype(BF16)
            return 0

        lax.fori_loop(0, seq // rows, body, 0)
        vx_scr[:, :hd] = v_ref[...]
        vx_scr[:, hd:] = jnp.ones((seq, hd), BF16)
        prep_q(qc_ref, 0, 0)

    q_all = q_scr[qi % 2]

    m_acc = jnp.full((nrow, hd), -jnp.inf, F32)
    for kb in range(seq // tk):
        s = lax.dot_general(q_all, k_scr[kb * tk:(kb + 1) * tk, :], (((1,), (1,)), ((), ())),
                            preferred_element_type=F32)
        s_scr[:, kb * tk:(kb + 1) * tk] = s
        for lt in range(tk // hd):
            m_acc = jnp.maximum(m_acc, s[:, lt * hd:(lt + 1) * hd])
    m = jnp.max(m_acc, axis=-1, keepdims=True)

    prep_q(qn_ref, jnp.minimum(qi + 1, nq - 1), (qi + 1) % 2)

    half = nrow // 2
    acc_lo = jnp.zeros((half, 2 * hd), F32)
    acc_hi = jnp.zeros((half, 2 * hd), F32)
    for kb in range(seq // tk):
        pb = jnp.exp2(s_scr[:, kb * tk:(kb + 1) * tk] - m).astype(BF16)
        vb = vx_scr[kb * tk:(kb + 1) * tk, :]
        acc_lo = acc_lo + jnp.dot(pb[:half], vb, preferred_element_type=F32)
        acc_hi = acc_hi + jnp.dot(pb[half:], vb, preferred_element_type=F32)
    acc = jnp.concatenate([acc_lo, acc_hi], axis=0)
    out = acc[:, :hd] / acc[:, hd:]
    for r in range(ATTN_REP):
        o_ref[:, r * hd:(r + 1) * hd] = out[r * tq:(r + 1) * tq].astype(o_ref.dtype)


def _attn(proj3, cos, sin, gq, gk, *, tq, tk):
    bsz, seq, _ = proj3.shape
    hd = ATTN_HEAD_DIM
    qw = ATTN_REP * hd
    nq = seq // tq
    return pl.pallas_call(
        functools.partial(_attn_kernel, tq=tq, tk=tk, rows=min(seq, 512)),
        grid=(bsz, ATTN_KV_HEADS, nq),
        in_specs=[
            pl.BlockSpec((None, tq, qw), lambda b, h, i: (b, i, COL_Q // qw + h)),
            pl.BlockSpec((None, tq, qw), lambda b, h, i: (b, jnp.minimum(i + 1, nq - 1), COL_Q // qw + h)),
            pl.BlockSpec((None, seq, hd), lambda b, h, i: (b, 0, COL_K // hd + h)),
            pl.BlockSpec((None, seq, hd), lambda b, h, i: (b, 0, COL_V // hd + h)),
            pl.BlockSpec((seq, hd), lambda b, h, i: (0, 0)),
            pl.BlockSpec((seq, hd), lambda b, h, i: (0, 0)),
            pl.BlockSpec((1, hd), lambda b, h, i: (0, 0)),
            pl.BlockSpec((1, hd), lambda b, h, i: (0, 0)),
        ],
        out_specs=pl.BlockSpec((None, tq, qw), lambda b, h, i: (b, i, h)),
        out_shape=jax.ShapeDtypeStruct((bsz, seq, ATTN_Q_HEADS * hd), BF16),
        scratch_shapes=[
            pltpu.VMEM((2, ATTN_REP * tq, hd), BF16),
            pltpu.VMEM((seq, hd), BF16),
            pltpu.VMEM((seq, 2 * hd), BF16),
            pltpu.VMEM((ATTN_REP * tq, seq), F32),
        ],
        compiler_params=_params("parallel", "parallel", "arbitrary"),
        name="attn",
    )(proj3, proj3, proj3, proj3, cos, sin, gq, gk)


def _merge_kernel(ys_ref, ya_ref, wa_ref, wb_ref, ga_ref, gb_ref, o_ref):
    a = jnp.dot(ys_ref[...], wa_ref[...], preferred_element_type=F32)
    b = jnp.dot(ya_ref[...], wb_ref[...], preferred_element_type=F32)
    merged = _sigmoid(ga_ref[...].astype(F32)) * a + _sigmoid(gb_ref[...].astype(F32)) * b
    o_ref[...] = merged.astype(o_ref.dtype)


def _merge(ys, ya, wa, wb, proj, *, tm, tn):
    m, d = ys.shape
    return pl.pallas_call(
        _merge_kernel,
        grid=(m // tm, D_MODEL // tn),
        in_specs=[
            pl.BlockSpec((tm, d), lambda i, j: (i, 0)),
            pl.BlockSpec((tm, d), lambda i, j: (i, 0)),
            pl.BlockSpec((d, tn), lambda i, j: (0, j)),
            pl.BlockSpec((d, tn), lambda i, j: (0, j)),
            pl.BlockSpec((tm, tn), lambda i, j: (i, COL_GA // tn + j)),
            pl.BlockSpec((tm, tn), lambda i, j: (i, COL_GB // tn + j)),
        ],
        out_specs=pl.BlockSpec((tm, tn), lambda i, j: (i, j)),
        out_shape=jax.ShapeDtypeStruct((m, D_MODEL), BF16),
        compiler_params=_params("parallel", "arbitrary"),
        name="merge",
    )(ys, ya, wa, wb, proj, proj)


def _outp_kernel(mg_ref, wo_ref, x_ref, g_ref, x1_ref, h2_ref):
    x1 = x_ref[...] + jnp.dot(mg_ref[...], wo_ref[...], preferred_element_type=F32)
    x1_ref[...] = x1
    ms = jnp.mean(x1 * x1, axis=-1, keepdims=True)
    h2_ref[...] = (x1 * lax.rsqrt(ms + EPS) * g_ref[...]).astype(h2_ref.dtype)


def _outp(merged, wo, x2d, g, *, tm):
    m, d = x2d.shape
    return pl.pallas_call(
        _outp_kernel,
        grid=(m // tm,),
        in_specs=[
            pl.BlockSpec((tm, d), lambda i: (i, 0)),
            pl.BlockSpec((d, d), lambda i: (0, 0)),
            pl.BlockSpec((tm, d), lambda i: (i, 0)),
            pl.BlockSpec((1, d), lambda i: (0, 0)),
        ],
        out_specs=[
            pl.BlockSpec((tm, d), lambda i: (i, 0)),
            pl.BlockSpec((tm, d), lambda i: (i, 0)),
        ],
        out_shape=[
            jax.ShapeDtypeStruct((m, d), F32),
            jax.ShapeDtypeStruct((m, d), BF16),
        ],
        compiler_params=_params("parallel"),
        name="outp",
    )(merged, wo, x2d, g)


def _mlp_kernel(h2_ref, w1_ref, w2_ref, x1_ref, g_ref, o_ref, *, final_norm):
    f = pl.program_id(1)

    @pl.when(f == 0)
    def _():
        o_ref[...] = x1_ref[...]

    act = jnp.maximum(jnp.dot(h2_ref[...], w1_ref[...], preferred_element_type=F32), 0.0)
    o_ref[...] += jnp.dot((act * act).astype(BF16), w2_ref[...], preferred_element_type=F32)

    if final_norm:
        @pl.when(f == pl.num_programs(1) - 1)
        def _():
            y = o_ref[...]
            ms = jnp.mean(y * y, axis=-1, keepdims=True)
            o_ref[...] = y * lax.rsqrt(ms + EPS) * g_ref[...]


def _mlp(h2, w1, w2, x1, g, *, tm, tf, final_norm):
    m, d = x1.shape
    ff = w1.shape[1]
    return pl.pallas_call(
        functools.partial(_mlp_kernel, final_norm=final_norm),
        grid=(m // tm, ff // tf),
        in_specs=[
            pl.BlockSpec((tm, d), lambda i, f: (i, 0)),
            pl.BlockSpec((d, tf), lambda i, f: (0, f)),
            pl.BlockSpec((tf, d), lambda i, f: (f, 0)),
            pl.BlockSpec((tm, d), lambda i, f: (i, 0)),
            pl.BlockSpec((1, d), lambda i, f: (0, 0)),
        ],
        out_specs=pl.BlockSpec((tm, d), lambda i, f: (i, 0)),
        out_shape=jax.ShapeDtypeStruct((m, d), F32),
        compiler_params=_params("parallel", "arbitrary"),
        name="mlp",
    )(h2, w1, w2, x1, g)


def _rope_tables(seq):
    rows = seq // GRID_W
    row = jnp.repeat(jnp.arange(rows, dtype=F32), GRID_W)
    col = jnp.tile(jnp.arange(GRID_W, dtype=F32), rows)
    half = ATTN_HEAD_DIM // 2
    inv_freq = 1.0 / (ROPE_THETA ** (jnp.arange(0, half, 2, dtype=F32) / half))
    ang = jnp.concatenate([row[:, None] * inv_freq, col[:, None] * inv_freq], axis=-1)
    cos = jnp.concatenate([jnp.cos(ang), jnp.cos(ang)], axis=-1)
    sin = jnp.concatenate([-jnp.sin(ang), jnp.sin(ang)], axis=-1)
    return cos, sin


def _deinterleave_heads(a, nheads):
    tail = a.shape[1:]
    a = a.reshape((nheads, ATTN_HEAD_DIM // 2, 2) + tail)
    return jnp.swapaxes(a, 1, 2).reshape((nheads * ATTN_HEAD_DIM,) + tail)


def _group_rows(v_f, v_b):
    g = jnp.concatenate([v_f.reshape(SSM_GROUPS, SSM_HPG), v_b.reshape(SSM_GROUPS, SSM_HPG)], axis=1)
    return g[:, :, None].astype(F32)


def kernel(x, g_mix, w_in, conv_w, conv_b, dt_bias_f, dt_bias_b, a_log_f, a_log_b, d_skip, ssm_norm_g,
           q_norm_g, k_norm_g, w_ssm_up, w_attn_up, w_out, g_mlp, w_mlp_in, w_mlp_out, g_final):
    bsz, seq, d = x.shape
    m = bsz * seq
    depth = w_in.shape[0]
    cos, sin = _rope_tables(seq)
    nchunks = seq // SSM_CHUNK
    tm_big = min(m, 1024)
    tm_mid = min(m, 512)

    x2d = x.reshape(m, d)
    for layer in range(depth):
        wt = w_in[layer].T
        tail = wt[COL_Q:].astype(BF16)
        q_lo = 2 * SSM_HEADS
        k_lo = q_lo + ATTN_Q_HEADS * ATTN_HEAD_DIM
        v_lo = k_lo + ATTN_KV_HEADS * ATTN_HEAD_DIM
        wb_t = jnp.concatenate([
            _deinterleave_heads(tail[q_lo:k_lo], ATTN_Q_HEADS),
            _deinterleave_heads(tail[k_lo:v_lo], ATTN_KV_HEADS),
            tail[v_lo:],
            tail[:q_lo],
            jnp.zeros((DT_PAD - 2 * SSM_HEADS, d), BF16)], axis=0)
        proj, dt = _proj(x2d, g_mix[layer][None, :], wt, wb_t, tm=tm_big, tn=1024)
        proj3 = proj.reshape(bsz, seq, N_MAIN)

        dt5 = dt[:, :2 * SSM_HEADS].reshape(bsz, nchunks, SSM_CHUNK, 2, SSM_GROUPS, SSM_HPG)
        dt5 = dt5.transpose(0, 4, 1, 3, 5, 2).reshape(bsz, SSM_GROUPS, nchunks, 2 * SSM_HPG, SSM_CHUNK)
        y_ssm = _ssd(
            proj3, dt5, conv_w[layer], conv_b[layer][None, :],
            _group_rows(dt_bias_f[layer], dt_bias_b[layer]),
            _group_rows(a_log_f[layer], a_log_b[layer]),
            jnp.repeat(d_skip[layer].astype(F32), SSM_HEAD_DIM)[None, :],
            ssm_norm_g[layer][None, :])

        y_attn = _attn(proj3, cos, sin, _deinterleave_heads(q_norm_g[layer], 1)[None, :],
                       _deinterleave_heads(k_norm_g[layer], 1)[None, :],
                       tq=min(seq, 512), tk=min(seq, 512))

        merged = _merge(y_ssm.reshape(m, SSM_D_INNER), y_attn.reshape(m, ATTN_Q_HEADS * ATTN_HEAD_DIM),
                        w_ssm_up[layer].astype(BF16), w_attn_up[layer].astype(BF16), proj,
                        tm=tm_big, tn=512)
        x1, h2 = _outp(merged, w_out[layer].astype(BF16), x2d, g_mlp[layer][None, :], tm=tm_mid)
        x2d = _mlp(h2, w_mlp_in[layer].astype(BF16), w_mlp_out[layer].astype(BF16), x1, g_final[None, :],
                   tm=tm_big, tf=512, final_norm=layer == depth - 1)
    return x2d.reshape(bsz, seq, d)
```

```python
import functools

import jax
import jax.numpy as jnp
import numpy as np
from jax import lax
from jax.experimental import pallas as pl
from jax.experimental.pallas import tpu as pltpu

F32 = jnp.float32
BF16 = jnp.bfloat16

EPS = 1e-6
LOG2E = 1.4426950408889634
D_MODEL = 2048
SSM_D_INNER = 2048
SSM_HEAD_DIM = 64
SSM_HEADS = SSM_D_INNER // SSM_HEAD_DIM
SSM_GROUPS = 8
SSM_HPG = SSM_HEADS // SSM_GROUPS
SSM_GW = SSM_D_INNER // SSM_GROUPS
SSM_STATE = 128
SSM_CONV_K = 5
SSM_CHUNK = 128
SSM_XBC = SSM_D_INNER + 2 * SSM_GROUPS * SSM_STATE
ATTN_HEAD_DIM = 128
ATTN_Q_HEADS = 16
ATTN_KV_HEADS = 4
ATTN_REP = ATTN_Q_HEADS // ATTN_KV_HEADS
ROPE_THETA = 10000.0
GRID_W = 64
D_FF = 4 * D_MODEL

COL_Z = 0
COL_XS = COL_Z + SSM_D_INNER
COL_B = COL_XS + SSM_D_INNER
COL_C = COL_B + SSM_GROUPS * SSM_STATE
COL_Q = COL_C + SSM_GROUPS * SSM_STATE
COL_K = COL_Q + ATTN_Q_HEADS * ATTN_HEAD_DIM
COL_V = COL_K + ATTN_KV_HEADS * ATTN_HEAD_DIM
COL_GA = COL_V + ATTN_KV_HEADS * ATTN_HEAD_DIM
COL_GB = COL_GA + D_MODEL
N_MAIN = COL_GB + D_MODEL
DT_PAD = 128

V7X_VMEM_LIMIT = 56 * 1024 * 1024
CONV_HALO = 16
CONV_WIN = SSM_CHUNK + 2 * CONV_HALO


def _params(*sem):
    return pltpu.CompilerParams(dimension_semantics=sem, vmem_limit_bytes=V7X_VMEM_LIMIT)


def _sigmoid(x):
    return 1.0 / (1.0 + jnp.exp(-x))


def _softplus(x):
    return jnp.maximum(x, 0.0) + jnp.log(1.0 + jnp.exp(-jnp.abs(x)))


_NT_DIMS = (((1,), (1,)), ((), ()))


def _proj_kernel(x_ref, g_ref, wa_ref, wb_ref, wdt_ref, o_ref, dt_ref, h_scr, *, rows, na):
    j = pl.program_id(1)

    @pl.when(j == 0)
    def _():
        def body(r, _):
            sl = pl.ds(pl.multiple_of(r * rows, rows), rows)
            x = x_ref[sl, :]
            ms = jnp.mean(x * x, axis=-1, keepdims=True)
            hb = (x * lax.rsqrt(ms + EPS) * g_ref[...]).astype(BF16)
            h_scr[sl, :] = hb
            dt_ref[sl, :] = lax.dot_general(hb, wdt_ref[...], _NT_DIMS, preferred_element_type=F32)
            return 0

        lax.fori_loop(0, x_ref.shape[0] // rows, body, 0)

    @pl.when(j < na)
    def _():
        o_ref[...] = lax.dot_general(h_scr[...], wa_ref[...], _NT_DIMS,
                                     preferred_element_type=F32).astype(o_ref.dtype)

    @pl.when(j >= na)
    def _():
        o_ref[...] = lax.dot_general(h_scr[...], wb_ref[...], _NT_DIMS,
                                     preferred_element_type=F32).astype(o_ref.dtype)


def _proj(x2d, g, wa_t, wb_t, *, tm, tn):
    m, d = x2d.shape
    n = N_MAIN
    na = COL_Q // tn
    return pl.pallas_call(
        functools.partial(_proj_kernel, rows=min(tm, 256), na=na),
        grid=(m // tm, n // tn),
        in_specs=[
            pl.BlockSpec((tm, d), lambda i, j: (i, 0)),
            pl.BlockSpec((1, d), lambda i, j: (0, 0)),
            pl.BlockSpec((tn, d), lambda i, j: (jnp.minimum(j, na - 1), 0)),
            pl.BlockSpec((tn, d), lambda i, j: (jnp.maximum(j - na, 0), 0)),
            pl.BlockSpec((DT_PAD, d), lambda i, j: ((N_MAIN - COL_Q) // DT_PAD, 0)),
        ],
        out_specs=[
            pl.BlockSpec((tm, tn), lambda i, j: (i, j)),
            pl.BlockSpec((tm, DT_PAD), lambda i, j: (i, 0)),
        ],
        out_shape=[
            jax.ShapeDtypeStruct((m, n), BF16),
            jax.ShapeDtypeStruct((m, DT_PAD), F32),
        ],
        scratch_shapes=[pltpu.VMEM((tm, d), BF16)],
        compiler_params=_params("parallel", "arbitrary"),
        name="proj",
    )(x2d, g, wa_t, wb_t, wb_t)


_CONV_SIDE_TAPS = tuple(k for k in range(SSM_CONV_K) if k != SSM_CONV_K // 2)


def _conv_window(c, seq):
    L = SSM_CHUNK
    start = jnp.clip(c * L - CONV_HALO, 0, seq - CONV_WIN)
    start = pl.multiple_of(start, CONV_HALO)
    return start, (c * L - start) // CONV_HALO


def _shift_matrix(halo_units):
    L = SSM_CHUNK
    t_idx = lax.broadcasted_iota(jnp.int32, (L, CONV_WIN), 0)
    j_idx = lax.broadcasted_iota(jnp.int32, (L, CONV_WIN), 1)
    base = j_idx - t_idx - halo_units * CONV_HALO
    shifts = [jnp.where(base == k - SSM_CONV_K // 2, 1.0, 0.0).astype(BF16) for k in _CONV_SIDE_TAPS]
    return jnp.concatenate(shifts, axis=0)


def _conv_silu(shifted, centre, w_ref, b_ref):
    L = SSM_CHUNK
    mid = SSM_CONV_K // 2
    acc = centre * w_ref[mid:mid + 1, :] + b_ref[...]
    for n, k in enumerate(_CONV_SIDE_TAPS):
        acc = acc + shifted[n * L:(n + 1) * L] * w_ref[k:k + 1, :]
    return acc * _sigmoid(acc)


def _split3(a):
    a1 = a.astype(BF16)
    r1 = a - a1.astype(F32)
    a2 = r1.astype(BF16)
    a3 = (r1 - a2.astype(F32)).astype(BF16)
    return a1, a2, a3


def _ssd_kernel(z_ref, xs_ref, b_ref, c_ref, dt_ref, cwx_ref, cwb_ref, cwc_ref, cbx_ref, cbb_ref,
                cbc_ref, dtb_ref, alog_ref, dsk_ref, ng_ref, y_ref,
                shift_scr, x_scr, bt_scr, c_scr, rows_scr, cols_scr, yf_scr, yb_scr, hf_scr, hb_scr):
    L = SSM_CHUNK
    H = SSM_HPG
    N = SSM_STATE
    seq = xs_ref.shape[0]
    nchunks = seq // L

    row_i = lax.broadcasted_iota(jnp.int32, (L, L), 0)
    lane_i = lax.broadcasted_iota(jnp.int32, (L, L), 1)
    tri_pre = jnp.where(row_i <= lane_i, 1.0, 0.0).astype(BF16)
    tri_suf = jnp.where(row_i >= lane_i, 1.0, 0.0).astype(BF16)
    neg_a = -jnp.exp(alog_ref[...]) * LOG2E
    lane128 = lax.broadcasted_iota(jnp.int32, (L, L), 1)

    def col_bcast(cols, lane):
        return jnp.broadcast_to(cols[:, lane:lane + 1], (L, L))

    def expand(bc):
        lo = jnp.where(lane128 < SSM_HEAD_DIM, bc[0], bc[1])
        hi = jnp.where(lane128 < SSM_HEAD_DIM, bc[2], bc[3])
        return jnp.concatenate([lo, hi], axis=1)

    for halo_units in range(3):
        shift_scr[halo_units] = _shift_matrix(halo_units)

    def prep(c, _):
        rows_sl = pl.ds(pl.multiple_of(c * L, L), L)
        start, halo_units = _conv_window(c, seq)
        shift = shift_scr[halo_units]
        win = pl.ds(start, CONV_WIN)
        sx = jnp.dot(shift, xs_ref[win, :], preferred_element_type=F32)
        sbc = jnp.dot(shift, jnp.concatenate([b_ref[win, :], c_ref[win, :]], axis=1),
                      preferred_element_type=F32)
        xc = _conv_silu(sx, xs_ref[rows_sl, :].astype(F32), cwx_ref, cbx_ref)
        x_scr[rows_sl, :] = xc
        bm = _conv_silu(sbc[:, :N], b_ref[rows_sl, :].astype(F32), cwb_ref, cbb_ref)
        bt_scr[c] = bm.T.astype(BF16)
        c_scr[rows_sl, :] = _conv_silu(sbc[:, N:], c_ref[rows_sl, :].astype(F32), cwc_ref, cbc_ref).astype(BF16)

        dt = _softplus(dt_ref[c] + dtb_ref[...])
        a = neg_a * dt
        a3 = jnp.concatenate(_split3(a), axis=0)
        pre = jnp.dot(a3, tri_pre, preferred_element_type=F32)
        suf = jnp.dot(a3, tri_suf, preferred_element_type=F32)
        pre = pre[0:2 * H] + pre[2 * H:4 * H] + pre[4 * H:6 * H]
        suf = suf[0:2 * H] + suf[2 * H:4 * H] + suf[4 * H:6 * H]
        is_fwd = lax.broadcasted_iota(jnp.int32, (2 * H, L), 0) < H
        rows = jnp.concatenate([jnp.where(is_fwd, pre, suf), dt], axis=0)
        rows_scr[c] = rows
        padded = jnp.concatenate([rows, jnp.zeros((L - 4 * H, L), F32)], axis=0)
        cols_scr[rows_sl, :] = padded.T
        return 0

    lax.fori_loop(0, nchunks, prep, 0, unroll=8)

    row_h = lax.broadcasted_iota(jnp.int32, (H * L, SSM_GW), 0) // L
    lane_h = lax.broadcasted_iota(jnp.int32, (H * L, SSM_GW), 1) // SSM_HEAD_DIM

    def chunk(c, d, h_scr):
        rows_sl = pl.ds(pl.multiple_of(c * L, L), L)
        rows = rows_scr[c]
        cols = cols_scr[rows_sl, :]
        xc = x_scr[rows_sl, :]
        cm = c_scr[rows_sl, :]
        bt = bt_scr[c]
        scores = jnp.dot(cm, bt, preferred_element_type=F32)
        scores = jnp.where((lane_i <= row_i) if d == 0 else (lane_i >= row_i), scores, 0.0)
        cum_bc = [col_bcast(cols, d * H + h) for h in range(H)]
        dt_e = jnp.concatenate(
            [jnp.take_along_axis(cols, 2 * H + d * H + 2 * half + lane128 // SSM_HEAD_DIM, axis=1)
             for half in range(2)], axis=1)
        seg = (jnp.concatenate(cum_bc, axis=1)
               - jnp.concatenate([rows[d * H + h:d * H + h + 1, :] for h in range(H)], axis=1))
        decay = jnp.exp2(jnp.minimum(seg, 0.0))
        mix = jnp.concatenate([scores.astype(BF16)] * H, axis=1) * decay.astype(BF16)
        cum_e = expand(cum_bc)
        xdt = xc * dt_e
        xdt_b = xdt.astype(BF16)
        x_bd = jnp.where(row_h == lane_h, jnp.concatenate([xdt_b] * H, axis=0), jnp.zeros((), BF16))
        y = jnp.dot(mix, x_bd, preferred_element_type=F32)
        state = h_scr[...]
        y = y + jnp.dot(cm, state.astype(BF16), preferred_element_type=F32) * jnp.exp2(cum_e)
        tot = cum_e[L - 1:L, :] if d == 0 else cum_e[0:1, :]
        xe = (xdt * jnp.exp2(tot - cum_e)).astype(BF16)
        h_scr[...] = jnp.exp2(tot) * state + jnp.dot(bt, xe, preferred_element_type=F32)
        return y

    hf_scr[...] = jnp.zeros_like(hf_scr)
    hb_scr[...] = jnp.zeros_like(hb_scr)

    def scan(i, _):
        j = nchunks - 1 - i
        yf_scr[pl.ds(pl.multiple_of(i * L, L), L), :] = chunk(i, 0, hf_scr)
        yb_scr[pl.ds(pl.multiple_of(j * L, L), L), :] = chunk(j, 1, hb_scr)
        return 0

    lax.fori_loop(0, nchunks, scan, 0, unroll=8)

    def finish(c, _):
        rows_sl = pl.ds(pl.multiple_of(c * L, L), L)
        z = z_ref[rows_sl, :].astype(F32)
        y = yf_scr[rows_sl, :] + yb_scr[rows_sl, :] + dsk_ref[...] * x_scr[rows_sl, :]
        y = y * (z * _sigmoid(z))
        ms = jnp.mean(y * y, axis=-1, keepdims=True)
        y_ref[rows_sl, :] = (y * lax.rsqrt(ms + EPS) * ng_ref[...]).astype(y_ref.dtype)
        return 0

    lax.fori_loop(0, nchunks, finish, 0, unroll=4)


def _ssd(proj3, dt5, conv_w, conv_b, dtb, alog, dsk, ng):
    bsz, seq, _ = proj3.shape
    L, G, GW, N = SSM_CHUNK, SSM_GROUPS, SSM_GW, SSM_STATE
    nchunks = seq // L
    k = SSM_CONV_K
    col = lambda base, w: (lambda b, g: (b, 0, base // w + g))
    pcol = lambda base, w: (lambda b, g: (0, base // w + g))
    return pl.pallas_call(
        _ssd_kernel,
        grid=(bsz, G),
        in_specs=[
            pl.BlockSpec((None, seq, GW), col(COL_Z, GW)),
            pl.BlockSpec((None, seq, GW), col(COL_XS, GW)),
            pl.BlockSpec((None, seq, N), col(COL_B, N)),
            pl.BlockSpec((None, seq, N), col(COL_C, N)),
            pl.BlockSpec((None, None, nchunks, 2 * SSM_HPG, L), lambda b, g: (b, g, 0, 0, 0)),
            pl.BlockSpec((k, GW), pcol(0, GW)),
            pl.BlockSpec((k, N), pcol(SSM_D_INNER, N)),
            pl.BlockSpec((k, N), pcol(SSM_D_INNER + G * N, N)),
            pl.BlockSpec((1, GW), pcol(0, GW)),
            pl.BlockSpec((1, N), pcol(SSM_D_INNER, N)),
            pl.BlockSpec((1, N), pcol(SSM_D_INNER + G * N, N)),
            pl.BlockSpec((None, 2 * SSM_HPG, 1), lambda b, g: (g, 0, 0)),
            pl.BlockSpec((None, 2 * SSM_HPG, 1), lambda b, g: (g, 0, 0)),
            pl.BlockSpec((1, GW), pcol(0, GW)),
            pl.BlockSpec((1, GW), pcol(0, GW)),
        ],
        out_specs=pl.BlockSpec((None, seq, GW), lambda b, g: (b, 0, g)),
        out_shape=jax.ShapeDtypeStruct((bsz, seq, SSM_D_INNER), BF16),
        scratch_shapes=[
            pltpu.VMEM((3, len(_CONV_SIDE_TAPS) * L, CONV_WIN), BF16),
            pltpu.VMEM((seq, GW), F32),
            pltpu.VMEM((nchunks, N, L), BF16),
            pltpu.VMEM((seq, N), BF16),
            pltpu.VMEM((nchunks, 4 * SSM_HPG, L), F32),
            pltpu.VMEM((seq, L), F32),
            pltpu.VMEM((seq, GW), F32),
            pltpu.VMEM((seq, GW), F32),
            pltpu.VMEM((N, GW), F32),
            pltpu.VMEM((N, GW), F32),
        ],
        compiler_params=_params("parallel", "parallel"),
        name="ssd",
    )(proj3, proj3, proj3, proj3, dt5, conv_w, conv_w, conv_w, conv_b, conv_b, conv_b, dtb, alog, dsk, ng)


def _rms_rope(x, g, cos, sin):
    ms = jnp.mean(x * x, axis=-1, keepdims=True)
    x = x * lax.rsqrt(ms + EPS) * g
    return x * cos + pltpu.roll(x, ATTN_HEAD_DIM // 2, axis=1) * sin


def _attn_kernel(qc_ref, qn_ref, k_ref, v_ref, cos_ref, sin_ref, gq_ref, gk_ref, o_ref,
                       q_scr, k_scr, vx_scr, s_scr, *, tq, tk, rows):
    seq = k_ref.shape[0]
    hd = ATTN_HEAD_DIM
    nrow = ATTN_REP * tq
    qi = pl.program_id(2)
    nq = pl.num_programs(2)
    scale = hd ** -0.5 * LOG2E

    def prep_q(q_ref, tile, slot):
        sl = pl.ds(pl.multiple_of(tile * tq, tq), tq)
        cos = cos_ref[sl, :]
        sin = sin_ref[sl, :]
        for r in range(ATTN_REP):
            q = _rms_rope(q_ref[:, r * hd:(r + 1) * hd].astype(F32), gq_ref[...], cos, sin)
            q_scr[slot, r * tq:(r + 1) * tq, :] = (q * scale).astype(BF16)

    @pl.when(qi == 0)
    def _():
        def body(r, _):
            sl = pl.ds(pl.multiple_of(r * rows, rows), rows)
            k = _rms_rope(k_ref[sl, :].astype(F32), gk_ref[...], cos_ref[sl, :], sin_ref[sl, :])
            k_scr[sl, :] = k.astype(BF16)
            return 0

        lax.fori_loop(0, seq // rows, body, 0)
        vx_scr[:, :hd] = v_ref[...]
        vx_scr[:, hd:] = jnp.ones((seq, hd), BF16)
        prep_q(qc_ref, 0, 0)

    q_all = q_scr[qi % 2]

    m_acc = jnp.full((nrow, hd), -jnp.inf, F32)
    for kb in range(seq // tk):
        s = lax.dot_general(q_all, k_scr[kb * tk:(kb + 1) * tk, :], (((1,), (1,)), ((), ())),
                            preferred_element_type=F32)
        s_scr[:, kb * tk:(kb + 1) * tk] = s
        for lt in range(tk // hd):
            m_acc = jnp.maximum(m_acc, s[:, lt * hd:(lt + 1) * hd])
    m = jnp.max(m_acc, axis=-1, keepdims=True)

    prep_q(qn_ref, jnp.minimum(qi + 1, nq - 1), (qi + 1) % 2)

    half = nrow // 2
    acc_lo = jnp.zeros((half, 2 * hd), F32)
    acc_hi = jnp.zeros((half, 2 * hd), F32)
    for kb in range(seq // tk):
        pb = jnp.exp2(s_scr[:, kb * tk:(kb + 1) * tk] - m).astype(BF16)
        vb = vx_scr[kb * tk:(kb + 1) * tk, :]
        acc_lo = acc_lo + jnp.dot(pb[:half], vb, preferred_element_type=F32)
        acc_hi = acc_hi + jnp.dot(pb[half:], vb, preferred_element_type=F32)
    acc = jnp.concatenate([acc_lo, acc_hi], axis=0)
    out = acc[:, :hd] / acc[:, hd:]
    for r in range(ATTN_REP):
        o_ref[:, r * hd:(r + 1) * hd] = out[r * tq:(r + 1) * tq].astype(o_ref.dtype)


def _attn(proj3, cos, sin, gq, gk, *, tq, tk):
    bsz, seq, _ = proj3.shape
    hd = ATTN_HEAD_DIM
    qw = ATTN_REP * hd
    nq = seq // tq
    return pl.pallas_call(
        functools.partial(_attn_kernel, tq=tq, tk=tk, rows=min(seq, 512)),
        grid=(bsz, ATTN_KV_HEADS, nq),
        in_specs=[
            pl.BlockSpec((None, tq, qw), lambda b, h, i: (b, i, COL_Q // qw + h)),
            pl.BlockSpec((None, tq, qw), lambda b, h, i: (b, jnp.minimum(i + 1, nq - 1), COL_Q // qw + h)),
            pl.BlockSpec((None, seq, hd), lambda b, h, i: (b, 0, COL_K // hd + h)),
            pl.BlockSpec((None, seq, hd), lambda b, h, i: (b, 0, COL_V // hd + h)),
            pl.BlockSpec((seq, hd), lambda b, h, i: (0, 0)),
            pl.BlockSpec((seq, hd), lambda b, h, i: (0, 0)),
            pl.BlockSpec((1, hd), lambda b, h, i: (0, 0)),
            pl.BlockSpec((1, hd), lambda b, h, i: (0, 0)),
        ],
        out_specs=pl.BlockSpec((None, tq, qw), lambda b, h, i: (b, i, h)),
        out_shape=jax.ShapeDtypeStruct((bsz, seq, ATTN_Q_HEADS * hd), BF16),
        scratch_shapes=[
            pltpu.VMEM((2, ATTN_REP * tq, hd), BF16),
            pltpu.VMEM((seq, hd), BF16),
            pltpu.VMEM((seq, 2 * hd), BF16),
            pltpu.VMEM((ATTN_REP * tq, seq), F32),
        ],
        compiler_params=_params("parallel", "parallel", "arbitrary"),
        name="attn",
    )(proj3, proj3, proj3, proj3, cos, sin, gq, gk)


def _merge_kernel(ys_ref, ya_ref, wa_ref, wb_ref, ga_ref, gb_ref, o_ref):
    a = jnp.dot(ys_ref[...], wa_ref[...], preferred_element_type=F32)
    b = jnp.dot(ya_ref[...], wb_ref[...], preferred_element_type=F32)
    merged = _sigmoid(ga_ref[...].astype(F32)) * a + _sigmoid(gb_ref[...].astype(F32)) * b
    o_ref[...] = merged.astype(o_ref.dtype)


def _merge(ys, ya, wa, wb, proj, *, tm, tn):
    m, d = ys.shape
    return pl.pallas_call(
        _merge_kernel,
        grid=(m // tm, D_MODEL // tn),
        in_specs=[
            pl.BlockSpec((tm, d), lambda i, j: (i, 0)),
            pl.BlockSpec((tm, d), lambda i, j: (i, 0)),
            pl.BlockSpec((d, tn), lambda i, j: (0, j)),
            pl.BlockSpec((d, tn), lambda i, j: (0, j)),
            pl.BlockSpec((tm, tn), lambda i, j: (i, COL_GA // tn + j)),
            pl.BlockSpec((tm, tn), lambda i, j: (i, COL_GB // tn + j)),
        ],
        out_specs=pl.BlockSpec((tm, tn), lambda i, j: (i, j)),
        out_shape=jax.ShapeDtypeStruct((m, D_MODEL), BF16),
        compiler_params=_params("parallel", "arbitrary"),
        name="merge",
    )(ys, ya, wa, wb, proj, proj)


def _outp_kernel(mg_ref, wo_ref, x_ref, g_ref, x1_ref, h2_ref):
    x1 = x_ref[...] + jnp.dot(mg_ref[...], wo_ref[...], preferred_element_type=F32)
    x1_ref[...] = x1
    ms = jnp.mean(x1 * x1, axis=-1, keepdims=True)
    h2_ref[...] = (x1 * lax.rsqrt(ms + EPS) * g_ref[...]).astype(h2_ref.dtype)


def _outp(merged, wo, x2d, g, *, tm):
    m, d = x2d.shape
    return pl.pallas_call(
        _outp_kernel,
        grid=(m // tm,),
        in_specs=[
            pl.BlockSpec((tm, d), lambda i: (i, 0)),
            pl.BlockSpec((d, d), lambda i: (0, 0)),
            pl.BlockSpec((tm, d), lambda i: (i, 0)),
            pl.BlockSpec((1, d), lambda i: (0, 0)),
        ],
        out_specs=[
            pl.BlockSpec((tm, d), lambda i: (i, 0)),
            pl.BlockSpec((tm, d), lambda i: (i, 0)),
        ],
        out_shape=[
            jax.ShapeDtypeStruct((m, d), F32),
            jax.ShapeDtypeStruct((m, d), BF16),
        ],
        compiler_params=_params("parallel"),
        name="outp",
    )(merged, wo, x2d, g)


def _mlp_kernel(h2_ref, w1_ref, w2_ref, x1_ref, g_ref, o_ref, *, final_norm):
    f = pl.program_id(1)

    @pl.when(f == 0)
    def _():
        o_ref[...] = x1_ref[...]

    act = jnp.maximum(jnp.dot(h2_ref[...], w1_ref[...], preferred_element_type=F32), 0.0)
    o_ref[...] += jnp.dot((act * act).astype(BF16), w2_ref[...], preferred_element_type=F32)

    if final_norm:
        @pl.when(f == pl.num_programs(1) - 1)
        def _():
            y = o_ref[...]
            ms = jnp.mean(y * y, axis=-1, keepdims=True)
            o_ref[...] = y * lax.rsqrt(ms + EPS) * g_ref[...]


def _mlp(h2, w1, w2, x1, g, *, tm, tf, final_norm):
    m, d = x1.shape
    ff = w1.shape[1]
    return pl.pallas_call(
        functools.partial(_mlp_kernel, final_norm=final_norm),
        grid=(m // tm, ff // tf),
        in_specs=[
            pl.BlockSpec((tm, d), lambda i, f: (i, 0)),
            pl.BlockSpec((d, tf), lambda i, f: (0, f)),
            pl.BlockSpec((tf, d), lambda i, f: (f, 0)),
            pl.BlockSpec((tm, d), lambda i, f: (i, 0)),
            pl.BlockSpec((1, d), lambda i, f: (0, 0)),
        ],
        out_specs=pl.BlockSpec((tm, d), lambda i, f: (i, 0)),
        out_shape=jax.ShapeDtypeStruct((m, d), F32),
        compiler_params=_params("parallel", "arbitrary"),
        name="mlp",
    )(h2, w1, w2, x1, g)


def _rope_tables(seq):
    rows = seq // GRID_W
    row = jnp.repeat(jnp.arange(rows, dtype=F32), GRID_W)
    col = jnp.tile(jnp.arange(GRID_W, dtype=F32), rows)
    half = ATTN_HEAD_DIM // 2
    inv_freq = 1.0 / (ROPE_THETA ** (jnp.arange(0, half, 2, dtype=F32) / half))
    ang = jnp.concatenate([row[:, None] * inv_freq, col[:, None] * inv_freq], axis=-1)
    cos = jnp.concatenate([jnp.cos(ang), jnp.cos(ang)], axis=-1)
    sin = jnp.concatenate([-jnp.sin(ang), jnp.sin(ang)], axis=-1)
    return cos, sin


def _deinterleave_heads(a, nheads):
    tail = a.shape[1:]
    a = a.reshape((nheads, ATTN_HEAD_DIM // 2, 2) + tail)
    return jnp.swapaxes(a, 1, 2).reshape((nheads * ATTN_HEAD_DIM,) + tail)


def _group_rows(v_f, v_b):
    g = jnp.concatenate([v_f.reshape(SSM_GROUPS, SSM_HPG), v_b.reshape(SSM_GROUPS, SSM_HPG)], axis=1)
    return g[:, :, None].astype(F32)


def kernel(x, g_mix, w_in, conv_w, conv_b, dt_bias_f, dt_bias_b, a_log_f, a_log_b, d_skip, ssm_norm_g,
           q_norm_g, k_norm_g, w_ssm_up, w_attn_up, w_out, g_mlp, w_mlp_in, w_mlp_out, g_final):
    bsz, seq, d = x.shape
    m = bsz * seq
    depth = w_in.shape[0]
    cos, sin = _rope_tables(seq)
    nchunks = seq // SSM_CHUNK
    tm_big = min(m, 1024)
    tm_mid = min(m, 512)

    x2d = x.reshape(m, d)
    for layer in range(depth):
        wt = w_in[layer].T.astype(BF16)
        tail = wt[COL_Q:]
        q_lo = 2 * SSM_HEADS
        k_lo = q_lo + ATTN_Q_HEADS * ATTN_HEAD_DIM
        v_lo = k_lo + ATTN_KV_HEADS * ATTN_HEAD_DIM
        wb_t = jnp.concatenate([
            _deinterleave_heads(tail[q_lo:k_lo], ATTN_Q_HEADS),
            _deinterleave_heads(tail[k_lo:v_lo], ATTN_KV_HEADS),
            tail[v_lo:],
            tail[:q_lo],
            jnp.zeros((DT_PAD - 2 * SSM_HEADS, d), BF16)], axis=0)
        proj, dt = _proj(x2d, g_mix[layer][None, :], wt, wb_t, tm=tm_big, tn=1024)
        proj3 = proj.reshape(bsz, seq, N_MAIN)

        dt5 = dt[:, :2 * SSM_HEADS].reshape(bsz, nchunks, SSM_CHUNK, 2, SSM_GROUPS, SSM_HPG)
        dt5 = dt5.transpose(0, 4, 1, 3, 5, 2).reshape(bsz, SSM_GROUPS, nchunks, 2 * SSM_HPG, SSM_CHUNK)
        y_ssm = _ssd(
            proj3, dt5, conv_w[layer], conv_b[layer][None, :],
            _group_rows(dt_bias_f[layer], dt_bias_b[layer]),
            _group_rows(a_log_f[layer], a_log_b[layer]),
            jnp.repeat(d_skip[layer].astype(F32), SSM_HEAD_DIM)[None, :],
            ssm_norm_g[layer][None, :])

        y_attn = _attn(proj3, cos, sin, _deinterleave_heads(q_norm_g[layer], 1)[None, :],
                       _deinterleave_heads(k_norm_g[layer], 1)[None, :],
                       tq=min(seq, 512), tk=min(seq, 512))

        merged = _merge(y_ssm.reshape(m, SSM_D_INNER), y_attn.reshape(m, ATTN_Q_HEADS * ATTN_HEAD_DIM),
                        w_ssm_up[layer].astype(BF16), w_attn_up[layer].astype(BF16), proj,
                        tm=tm_big, tn=512)
        x1, h2 = _outp(merged, w_out[layer].astype(BF16), x2d, g_mlp[layer][None, :], tm=tm_mid)
        x2d = _mlp(h2, w_mlp_in[layer].astype(BF16), w_mlp_out[layer].astype(BF16), x1, g_final[None, :],
                   tm=tm_big, tf=512, final_norm=layer == depth - 1)
    return x2d.reshape(bsz, seq, d)
```

```python
import functools

import jax
import jax.numpy as jnp
from jax import lax
from jax.experimental import pallas as pl
from jax.experimental.pallas import tpu as pltpu

F32 = jnp.float32
BF16 = jnp.bfloat16

EPS = 1e-6
LOG2E = 1.4426950408889634
D_MODEL = 2048
SSM_D_INNER = 2048
SSM_HEAD_DIM = 64
SSM_HEADS = SSM_D_INNER // SSM_HEAD_DIM
SSM_GROUPS = 8
SSM_HPG = SSM_HEADS // SSM_GROUPS
SSM_GW = SSM_D_INNER // SSM_GROUPS
SSM_STATE = 128
SSM_CONV_K = 5
SSM_CHUNK = 128
SSM_XBC = SSM_D_INNER + 2 * SSM_GROUPS * SSM_STATE
ATTN_HEAD_DIM = 128
ATTN_Q_HEADS = 16
ATTN_KV_HEADS = 4
ATTN_REP = ATTN_Q_HEADS // ATTN_KV_HEADS
ROPE_THETA = 10000.0
GRID_W = 64
D_FF = 4 * D_MODEL

COL_Z = 0
COL_XS = COL_Z + SSM_D_INNER
COL_B = COL_XS + SSM_D_INNER
COL_C = COL_B + SSM_GROUPS * SSM_STATE
COL_Q = COL_C + SSM_GROUPS * SSM_STATE
COL_K = COL_Q + ATTN_Q_HEADS * ATTN_HEAD_DIM
COL_V = COL_K + ATTN_KV_HEADS * ATTN_HEAD_DIM
COL_GA = COL_V + ATTN_KV_HEADS * ATTN_HEAD_DIM
COL_GB = COL_GA + D_MODEL
N_MAIN = COL_GB + D_MODEL
DT_PAD = 128

V7X_VMEM_LIMIT = 56 * 1024 * 1024
CONV_HALO = 16
CONV_WIN = SSM_CHUNK + 2 * CONV_HALO


def _params(*sem):
    return pltpu.CompilerParams(dimension_semantics=sem, vmem_limit_bytes=V7X_VMEM_LIMIT)


def _sigmoid(x):
    return 1.0 / (1.0 + jnp.exp(-x))


def _softplus(x):
    return jnp.maximum(x, 0.0) + jnp.log(1.0 + jnp.exp(-jnp.abs(x)))


_NT_DIMS = (((1,), (1,)), ((), ()))


def _proj_kernel(x_ref, g_ref, wa_ref, wb_ref, wdt_ref, *rest, rows, na, ncast, nside):
    side_in, (o_ref, dt_ref), side_out, h_scr = rest[:nside], rest[nside:nside + 2], rest[nside + 2:-1], rest[-1]
    j = pl.program_id(1)

    @pl.when(pl.program_id(0) * pl.num_programs(1) + j < ncast)
    def _():
        for src, dst in zip(side_in, side_out):
            dst[...] = src[...].astype(dst.dtype)

    @pl.when(j == 0)
    def _():
        def body(r, _):
            sl = pl.ds(pl.multiple_of(r * rows, rows), rows)
            x = x_ref[sl, :]
            ms = jnp.mean(x * x, axis=-1, keepdims=True)
            hb = (x * lax.rsqrt(ms + EPS) * g_ref[...]).astype(BF16)
            h_scr[sl, :] = hb
            dt_ref[sl, :] = jnp.dot(hb, wdt_ref[...], preferred_element_type=F32)
            return 0

        lax.fori_loop(0, x_ref.shape[0] // rows, body, 0)

    @pl.when(j < na)
    def _():
        o_ref[...] = lax.dot_general(h_scr[...], wa_ref[...], _NT_DIMS,
                                     preferred_element_type=F32).astype(o_ref.dtype)

    @pl.when(j >= na)
    def _():
        o_ref[...] = jnp.dot(h_scr[...], wb_ref[...], preferred_element_type=F32).astype(o_ref.dtype)


MAX_CAST_SLICES = 128


def _proj(x2d, g, wa_t, wb, side, *, tm, tn):
    m, d = x2d.shape
    n = N_MAIN
    na = COL_Q // tn
    nj = n // tn
    steps = (m // tm) * nj
    ncast = min(MAX_CAST_SLICES, 1 << (steps.bit_length() - 1))
    slice_map = lambda i, j: (jnp.minimum(i * nj + j, ncast - 1), 0)
    side_specs = [pl.BlockSpec((w.shape[0] // ncast, w.shape[1]), slice_map) for w in side]
    return pl.pallas_call(
        functools.partial(_proj_kernel, rows=min(tm, 256), na=na, ncast=ncast, nside=len(side)),
        grid=(m // tm, nj),
        in_specs=[
            pl.BlockSpec((tm, d), lambda i, j: (i, 0)),
            pl.BlockSpec((1, d), lambda i, j: (0, 0)),
            pl.BlockSpec((tn, d), lambda i, j: (jnp.minimum(j, na - 1), 0)),
            pl.BlockSpec((d, tn), lambda i, j: (0, jnp.maximum(j - na, 0))),
            pl.BlockSpec((d, DT_PAD), lambda i, j: (0, (N_MAIN - COL_Q) // DT_PAD)),
        ] + side_specs,
        out_specs=[
            pl.BlockSpec((tm, tn), lambda i, j: (i, j)),
            pl.BlockSpec((tm, DT_PAD), lambda i, j: (i, 0)),
        ] + side_specs,
        out_shape=[
            jax.ShapeDtypeStruct((m, n), BF16),
            jax.ShapeDtypeStruct((m, DT_PAD), F32),
        ] + [jax.ShapeDtypeStruct(w.shape, BF16) for w in side],
        scratch_shapes=[pltpu.VMEM((tm, d), BF16)],
        compiler_params=_params("arbitrary", "arbitrary"),
        name="proj",
    )(x2d, g, wa_t, wb, wb, *side)


_CONV_SIDE_TAPS = tuple(k for k in range(SSM_CONV_K) if k != SSM_CONV_K // 2)


def _conv_window(c, seq):
    L = SSM_CHUNK
    start = jnp.clip(c * L - CONV_HALO, 0, seq - CONV_WIN)
    start = pl.multiple_of(start, CONV_HALO)
    return start, (c * L - start) // CONV_HALO


def _shift_matrix(halo_units):
    L = SSM_CHUNK
    t_idx = lax.broadcasted_iota(jnp.int32, (L, CONV_WIN), 0)
    j_idx = lax.broadcasted_iota(jnp.int32, (L, CONV_WIN), 1)
    base = j_idx - t_idx - halo_units * CONV_HALO
    shifts = [jnp.where(base == k - SSM_CONV_K // 2, 1.0, 0.0).astype(BF16) for k in _CONV_SIDE_TAPS]
    return jnp.concatenate(shifts, axis=0)


def _conv_silu(shifted, centre, w_ref, b_ref):
    L = SSM_CHUNK
    mid = SSM_CONV_K // 2
    acc = centre * w_ref[mid:mid + 1, :] + b_ref[...]
    for n, k in enumerate(_CONV_SIDE_TAPS):
        acc = acc + shifted[n * L:(n + 1) * L] * w_ref[k:k + 1, :]
    return acc * _sigmoid(acc)


def _split3(a):
    a1 = a.astype(BF16)
    r1 = a - a1.astype(F32)
    a2 = r1.astype(BF16)
    a3 = (r1 - a2.astype(F32)).astype(BF16)
    return a1, a2, a3


def _ssd_kernel(z_ref, xs_ref, b_ref, c_ref, dt_ref, cwx_ref, cwb_ref, cwc_ref, cbx_ref, cbb_ref,
                cbc_ref, dtb_ref, alog_ref, dsk_ref, ng_ref, y_ref,
                shift_scr, x_scr, bt_scr, c_scr, rows_scr, cols_scr, yf_scr, yb_scr, hf_scr, hb_scr):
    L = SSM_CHUNK
    H = SSM_HPG
    N = SSM_STATE
    seq = xs_ref.shape[0]
    nchunks = seq // L

    row_i = lax.broadcasted_iota(jnp.int32, (L, L), 0)
    lane_i = lax.broadcasted_iota(jnp.int32, (L, L), 1)
    tri_pre = jnp.where(row_i <= lane_i, 1.0, 0.0).astype(BF16)
    tri_suf = jnp.where(row_i >= lane_i, 1.0, 0.0).astype(BF16)
    neg_a = -jnp.exp(alog_ref[...]) * LOG2E
    lane128 = lax.broadcasted_iota(jnp.int32, (L, L), 1)

    def col_bcast(cols, lane):
        return jnp.broadcast_to(cols[:, lane:lane + 1], (L, L))

    def expand(bc):
        lo = jnp.where(lane128 < SSM_HEAD_DIM, bc[0], bc[1])
        hi = jnp.where(lane128 < SSM_HEAD_DIM, bc[2], bc[3])
        return jnp.concatenate([lo, hi], axis=1)

    for halo_units in range(3):
        shift_scr[halo_units] = _shift_matrix(halo_units)

    def prep(c, _):
        rows_sl = pl.ds(pl.multiple_of(c * L, L), L)
        start, halo_units = _conv_window(c, seq)
        shift = shift_scr[halo_units]
        win = pl.ds(start, CONV_WIN)
        sx = jnp.dot(shift, xs_ref[win, :], preferred_element_type=F32)
        sbc = jnp.dot(shift, jnp.concatenate([b_ref[win, :], c_ref[win, :]], axis=1),
                      preferred_element_type=F32)
        xc = _conv_silu(sx, xs_ref[rows_sl, :].astype(F32), cwx_ref, cbx_ref)
        x_scr[rows_sl, :] = xc
        bm = _conv_silu(sbc[:, :N], b_ref[rows_sl, :].astype(F32), cwb_ref, cbb_ref)
        bt_scr[c] = bm.T.astype(BF16)
        c_scr[rows_sl, :] = _conv_silu(sbc[:, N:], c_ref[rows_sl, :].astype(F32), cwc_ref, cbc_ref).astype(BF16)

        dt = _softplus(dt_ref[c] + dtb_ref[...])
        a = neg_a * dt
        a3 = jnp.concatenate(_split3(a), axis=0)
        pre = jnp.dot(a3, tri_pre, preferred_element_type=F32)
        suf = jnp.dot(a3, tri_suf, preferred_element_type=F32)
        pre = pre[0:2 * H] + pre[2 * H:4 * H] + pre[4 * H:6 * H]
        suf = suf[0:2 * H] + suf[2 * H:4 * H] + suf[4 * H:6 * H]
        is_fwd = lax.broadcasted_iota(jnp.int32, (2 * H, L), 0) < H
        rows = jnp.concatenate([jnp.where(is_fwd, pre, suf), dt], axis=0)
        rows_scr[c] = rows
        padded = jnp.concatenate([rows, jnp.zeros((L - 4 * H, L), F32)], axis=0)
        cols_scr[rows_sl, :] = padded.T
        return 0

    lax.fori_loop(0, nchunks, prep, 0, unroll=8)

    row_h = lax.broadcasted_iota(jnp.int32, (H * L, SSM_GW), 0) // L
    lane_h = lax.broadcasted_iota(jnp.int32, (H * L, SSM_GW), 1) // SSM_HEAD_DIM

    def chunk(c, d, h_scr):
        rows_sl = pl.ds(pl.multiple_of(c * L, L), L)
        rows = rows_scr[c]
        cols = cols_scr[rows_sl, :]
        xc = x_scr[rows_sl, :]
        cm = c_scr[rows_sl, :]
        bt = bt_scr[c]
        scores = jnp.dot(cm, bt, preferred_element_type=F32)
        scores = jnp.where((lane_i <= row_i) if d == 0 else (lane_i >= row_i), scores, 0.0)
        cum_bc = [col_bcast(cols, d * H + h) for h in range(H)]
        dt_e = jnp.concatenate(
            [jnp.take_along_axis(cols, 2 * H + d * H + 2 * half + lane128 // SSM_HEAD_DIM, axis=1)
             for half in range(2)], axis=1)
        seg = (jnp.concatenate(cum_bc, axis=1)
               - jnp.concatenate([rows[d * H + h:d * H + h + 1, :] for h in range(H)], axis=1))
        decay = jnp.exp2(jnp.minimum(seg, 0.0))
        mix = jnp.concatenate([scores.astype(BF16)] * H, axis=1) * decay.astype(BF16)
        cum_e = expand(cum_bc)
        xdt = xc * dt_e
        xdt_b = xdt.astype(BF16)
        x_bd = jnp.where(row_h == lane_h, jnp.concatenate([xdt_b] * H, axis=0), jnp.zeros((), BF16))
        y = jnp.dot(mix, x_bd, preferred_element_type=F32)
        state = h_scr[...]
        y = y + jnp.dot(cm, state.astype(BF16), preferred_element_type=F32) * jnp.exp2(cum_e)
        tot = cum_e[L - 1:L, :] if d == 0 else cum_e[0:1, :]
        xe = (xdt * jnp.exp2(tot - cum_e)).astype(BF16)
        h_scr[...] = jnp.exp2(tot) * state + jnp.dot(bt, xe, preferred_element_type=F32)
        return y

    hf_scr[...] = jnp.zeros_like(hf_scr)
    hb_scr[...] = jnp.zeros_like(hb_scr)

    def scan(i, _):
        j = nchunks - 1 - i
        yf_scr[pl.ds(pl.multiple_of(i * L, L), L), :] = chunk(i, 0, hf_scr)
        yb_scr[pl.ds(pl.multiple_of(j * L, L), L), :] = chunk(j, 1, hb_scr)
        return 0

    lax.fori_loop(0, nchunks, scan, 0, unroll=8)

    def finish(c, _):
        rows_sl = pl.ds(pl.multiple_of(c * L, L), L)
        z = z_ref[rows_sl, :].astype(F32)
        y = yf_scr[rows_sl, :] + yb_scr[rows_sl, :] + dsk_ref[...] * x_scr[rows_sl, :]
        y = y * (z * _sigmoid(z))
        ms = jnp.mean(y * y, axis=-1, keepdims=True)
        y_ref[rows_sl, :] = (y * lax.rsqrt(ms + EPS) * ng_ref[...]).astype(y_ref.dtype)
        return 0

    lax.fori_loop(0, nchunks, finish, 0, unroll=4)


def _ssd(proj3, dt5, conv_w, conv_b, dtb, alog, dsk, ng):
    bsz, seq, _ = proj3.shape
    L, G, GW, N = SSM_CHUNK, SSM_GROUPS, SSM_GW, SSM_STATE
    nchunks = seq // L
    k = SSM_CONV_K
    col = lambda base, w: (lambda b, g: (b, 0, base // w + g))
    pcol = lambda base, w: (lambda b, g: (0, base // w + g))
    return pl.pallas_call(
        _ssd_kernel,
        grid=(bsz, G),
        in_specs=[
            pl.BlockSpec((None, seq, GW), col(COL_Z, GW)),
            pl.BlockSpec((None, seq, GW), col(COL_XS, GW)),
            pl.BlockSpec((None, seq, N), col(COL_B, N)),
            pl.BlockSpec((None, seq, N), col(COL_C, N)),
            pl.BlockSpec((None, None, nchunks, 2 * SSM_HPG, L), lambda b, g: (b, g, 0, 0, 0)),
            pl.BlockSpec((k, GW), pcol(0, GW)),
            pl.BlockSpec((k, N), pcol(SSM_D_INNER, N)),
            pl.BlockSpec((k, N), pcol(SSM_D_INNER + G * N, N)),
            pl.BlockSpec((1, GW), pcol(0, GW)),
            pl.BlockSpec((1, N), pcol(SSM_D_INNER, N)),
            pl.BlockSpec((1, N), pcol(SSM_D_INNER + G * N, N)),
            pl.BlockSpec((None, 2 * SSM_HPG, 1), lambda b, g: (g, 0, 0)),
            pl.BlockSpec((None, 2 * SSM_HPG, 1), lambda b, g: (g, 0, 0)),
            pl.BlockSpec((1, GW), pcol(0, GW)),
            pl.BlockSpec((1, GW), pcol(0, GW)),
        ],
        out_specs=pl.BlockSpec((None, seq, GW), lambda b, g: (b, 0, g)),
        out_shape=jax.ShapeDtypeStruct((bsz, seq, SSM_D_INNER), BF16),
        scratch_shapes=[
            pltpu.VMEM((3, len(_CONV_SIDE_TAPS) * L, CONV_WIN), BF16),
            pltpu.VMEM((seq, GW), F32),
            pltpu.VMEM((nchunks, N, L), BF16),
            pltpu.VMEM((seq, N), BF16),
            pltpu.VMEM((nchunks, 4 * SSM_HPG, L), F32),
            pltpu.VMEM((seq, L), F32),
            pltpu.VMEM((seq, GW), F32),
            pltpu.VMEM((seq, GW), F32),
            pltpu.VMEM((N, GW), F32),
            pltpu.VMEM((N, GW), F32),
        ],
        compiler_params=_params("parallel", "parallel"),
        name="ssd",
    )(proj3, proj3, proj3, proj3, dt5, conv_w, conv_w, conv_w, conv_b, conv_b, conv_b, dtb, alog, dsk, ng)


def _rms_rope(x, g, cos, sin):
    ms = jnp.mean(x * x, axis=-1, keepdims=True)
    x = x * lax.rsqrt(ms + EPS) * g
    return x * cos + pltpu.roll(x, ATTN_HEAD_DIM // 2, axis=1) * sin


def _attn_kernel(qc_ref, qn_ref, k_ref, v_ref, cos_ref, sin_ref, gq_ref, gk_ref, o_ref,
                       q_scr, k_scr, vx_scr, s_scr, *, tq, tk, rows):
    seq = k_ref.shape[0]
    hd = ATTN_HEAD_DIM
    nrow = ATTN_REP * tq
    qi = pl.program_id(2)
    nq = pl.num_programs(2)
    scale = hd ** -0.5 * LOG2E

    def prep_q(q_ref, tile, slot):
        sl = pl.ds(pl.multiple_of(tile * tq, tq), tq)
        cos = cos_ref[sl, :]
        sin = sin_ref[sl, :]
        for r in range(ATTN_REP):
            q = _rms_rope(q_ref[:, r * hd:(r + 1) * hd].astype(F32), gq_ref[...], cos, sin)
            q_scr[slot, r * tq:(r + 1) * tq, :] = (q * scale).astype(BF16)

    @pl.when(qi == 0)
    def _():
        def body(r, _):
            sl = pl.ds(pl.multiple_of(r * rows, rows), rows)
            k = _rms_rope(k_ref[sl, :].astype(F32), gk_ref[...], cos_ref[sl, :], sin_ref[sl, :])
            k_scr[sl, :] = k.astype(BF16)
            return 0

        lax.fori_loop(0, seq // rows, body, 0)
        vx_scr[:, :hd] = v_ref[...]
        vx_scr[:, hd:] = jnp.ones((seq, hd), BF16)
        prep_q(qc_ref, 0, 0)

    q_all = q_scr[qi % 2]

    m_acc = jnp.full((nrow, hd), -jnp.inf, F32)
    for kb in range(seq // tk):
        s = lax.dot_general(q_all, k_scr[kb * tk:(kb + 1) * tk, :], (((1,), (1,)), ((), ())),
                            preferred_element_type=F32)
        s_scr[:, kb * tk:(kb + 1) * tk] = s
        for lt in range(tk // hd):
            m_acc = jnp.maximum(m_acc, s[:, lt * hd:(lt + 1) * hd])
    m = jnp.max(m_acc, axis=-1, keepdims=True)

    prep_q(qn_ref, jnp.minimum(qi + 1, nq - 1), (qi + 1) % 2)

    half = nrow // 2
    acc_lo = jnp.zeros((half, 2 * hd), F32)
    acc_hi = jnp.zeros((half, 2 * hd), F32)
    for kb in range(seq // tk):
        pb = jnp.exp2(s_scr[:, kb * tk:(kb + 1) * tk] - m).astype(BF16)
        vb = vx_scr[kb * tk:(kb + 1) * tk, :]
        acc_lo = acc_lo + jnp.dot(pb[:half], vb, preferred_element_type=F32)
        acc_hi = acc_hi + jnp.dot(pb[half:], vb, preferred_element_type=F32)
    acc = jnp.concatenate([acc_lo, acc_hi], axis=0)
    out = acc[:, :hd] / acc[:, hd:]
    for r in range(ATTN_REP):
        o_ref[:, r * hd:(r + 1) * hd] = out[r * tq:(r + 1) * tq].astype(o_ref.dtype)


def _attn(proj3, cos, sin, gq, gk, *, tq, tk):
    bsz, seq, _ = proj3.shape
    hd = ATTN_HEAD_DIM
    qw = ATTN_REP * hd
    nq = seq // tq
    return pl.pallas_call(
        functools.partial(_attn_kernel, tq=tq, tk=tk, rows=min(seq, 512)),
        grid=(bsz, ATTN_KV_HEADS, nq),
        in_specs=[
            pl.BlockSpec((None, tq, qw), lambda b, h, i: (b, i, COL_Q // qw + h)),
            pl.BlockSpec((None, tq, qw), lambda b, h, i: (b, jnp.minimum(i + 1, nq - 1), COL_Q // qw + h)),
            pl.BlockSpec((None, seq, hd), lambda b, h, i: (b, 0, COL_K // hd + h)),
            pl.BlockSpec((None, seq, hd), lambda b, h, i: (b, 0, COL_V // hd + h)),
            pl.BlockSpec((seq, hd), lambda b, h, i: (0, 0)),
            pl.BlockSpec((seq, hd), lambda b, h, i: (0, 0)),
            pl.BlockSpec((1, hd), lambda b, h, i: (0, 0)),
            pl.BlockSpec((1, hd), lambda b, h, i: (0, 0)),
        ],
        out_specs=pl.BlockSpec((None, tq, qw), lambda b, h, i: (b, i, h)),
        out_shape=jax.ShapeDtypeStruct((bsz, seq, ATTN_Q_HEADS * hd), BF16),
        scratch_shapes=[
            pltpu.VMEM((2, ATTN_REP * tq, hd), BF16),
            pltpu.VMEM((seq, hd), BF16),
            pltpu.VMEM((seq, 2 * hd), BF16),
            pltpu.VMEM((ATTN_REP * tq, seq), F32),
        ],
        compiler_params=_params("parallel", "parallel", "arbitrary"),
        name="attn",
    )(proj3, proj3, proj3, proj3, cos, sin, gq, gk)


def _merge_kernel(ys_ref, ya_ref, wa_ref, wb_ref, ga_ref, gb_ref, o_ref):
    a = jnp.dot(ys_ref[...], wa_ref[...], preferred_element_type=F32)
    b = jnp.dot(ya_ref[...], wb_ref[...], preferred_element_type=F32)
    merged = _sigmoid(ga_ref[...].astype(F32)) * a + _sigmoid(gb_ref[...].astype(F32)) * b
    o_ref[...] = merged.astype(o_ref.dtype)


def _merge(ys, ya, wa, wb, proj, *, tm, tn):
    m, d = ys.shape
    return pl.pallas_call(
        _merge_kernel,
        grid=(m // tm, D_MODEL // tn),
        in_specs=[
            pl.BlockSpec((tm, d), lambda i, j: (i, 0)),
            pl.BlockSpec((tm, d), lambda i, j: (i, 0)),
            pl.BlockSpec((d, tn), lambda i, j: (0, j)),
            pl.BlockSpec((d, tn), lambda i, j: (0, j)),
            pl.BlockSpec((tm, tn), lambda i, j: (i, COL_GA // tn + j)),
            pl.BlockSpec((tm, tn), lambda i, j: (i, COL_GB // tn + j)),
        ],
        out_specs=pl.BlockSpec((tm, tn), lambda i, j: (i, j)),
        out_shape=jax.ShapeDtypeStruct((m, D_MODEL), BF16),
        compiler_params=_params("parallel", "arbitrary"),
        name="merge",
    )(ys, ya, wa, wb, proj, proj)


def _outp_kernel(mg_ref, wo_ref, x_ref, g_ref, x1_ref, h2_ref):
    x1 = x_ref[...] + jnp.dot(mg_ref[...], wo_ref[...], preferred_element_type=F32)
    x1_ref[...] = x1
    ms = jnp.mean(x1 * x1, axis=-1, keepdims=True)
    h2_ref[...] = (x1 * lax.rsqrt(ms + EPS) * g_ref[...]).astype(h2_ref.dtype)


def _outp(merged, wo, x2d, g, *, tm):
    m, d = x2d.shape
    return pl.pallas_call(
        _outp_kernel,
        grid=(m // tm,),
        in_specs=[
            pl.BlockSpec((tm, d), lambda i: (i, 0)),
            pl.BlockSpec((d, d), lambda i: (0, 0)),
            pl.BlockSpec((tm, d), lambda i: (i, 0)),
            pl.BlockSpec((1, d), lambda i: (0, 0)),
        ],
        out_specs=[
            pl.BlockSpec((tm, d), lambda i: (i, 0)),
            pl.BlockSpec((tm, d), lambda i: (i, 0)),
        ],
        out_shape=[
            jax.ShapeDtypeStruct((m, d), F32),
            jax.ShapeDtypeStruct((m, d), BF16),
        ],
        compiler_params=_params("parallel"),
        name="outp",
    )(merged, wo, x2d, g)


def _mlp_kernel(h2_ref, w1_ref, w2_ref, x1_ref, g_ref, o_ref, *, final_norm):
    f = pl.program_id(1)

    @pl.when(f == 0)
    def _():
        o_ref[...] = x1_ref[...]

    act = jnp.maximum(jnp.dot(h2_ref[...], w1_ref[...], preferred_element_type=F32), 0.0)
    o_ref[...] += jnp.dot((act * act).astype(BF16), w2_ref[...], preferred_element_type=F32)

    if final_norm:
        @pl.when(f == pl.num_programs(1) - 1)
        def _():
            y = o_ref[...]
            ms = jnp.mean(y * y, axis=-1, keepdims=True)
            o_ref[...] = y * lax.rsqrt(ms + EPS) * g_ref[...]


def _mlp(h2, w1, w2, x1, g, *, tm, tf, final_norm):
    m, d = x1.shape
    ff = w1.shape[1]
    return pl.pallas_call(
        functools.partial(_mlp_kernel, final_norm=final_norm),
        grid=(m // tm, ff // tf),
        in_specs=[
            pl.BlockSpec((tm, d), lambda i, f: (i, 0)),
            pl.BlockSpec((d, tf), lambda i, f: (0, f)),
            pl.BlockSpec((tf, d), lambda i, f: (f, 0)),
            pl.BlockSpec((tm, d), lambda i, f: (i, 0)),
            pl.BlockSpec((1, d), lambda i, f: (0, 0)),
        ],
        out_specs=pl.BlockSpec((tm, d), lambda i, f: (i, 0)),
        out_shape=jax.ShapeDtypeStruct((m, d), F32),
        compiler_params=_params("parallel", "arbitrary"),
        name="mlp",
    )(h2, w1, w2, x1, g)


def _rope_tables(seq):
    rows = seq // GRID_W
    row = jnp.repeat(jnp.arange(rows, dtype=F32), GRID_W)
    col = jnp.tile(jnp.arange(GRID_W, dtype=F32), rows)
    half = ATTN_HEAD_DIM // 2
    inv_freq = 1.0 / (ROPE_THETA ** (jnp.arange(0, half, 2, dtype=F32) / half))
    ang = jnp.concatenate([row[:, None] * inv_freq, col[:, None] * inv_freq], axis=-1)
    cos = jnp.concatenate([jnp.cos(ang), jnp.cos(ang)], axis=-1)
    sin = jnp.concatenate([-jnp.sin(ang), jnp.sin(ang)], axis=-1)
    return cos, sin


def _deinterleave_heads(a, nheads):
    tail = a.shape[1:]
    a = a.reshape((nheads, ATTN_HEAD_DIM // 2, 2) + tail)
    return jnp.swapaxes(a, 1, 2).reshape((nheads * ATTN_HEAD_DIM,) + tail)


def _group_rows(v_f, v_b):
    g = jnp.concatenate([v_f.reshape(SSM_GROUPS, SSM_HPG), v_b.reshape(SSM_GROUPS, SSM_HPG)], axis=1)
    return g[:, :, None].astype(F32)


def kernel(x, g_mix, w_in, conv_w, conv_b, dt_bias_f, dt_bias_b, a_log_f, a_log_b, d_skip, ssm_norm_g,
           q_norm_g, k_norm_g, w_ssm_up, w_attn_up, w_out, g_mlp, w_mlp_in, w_mlp_out, g_final):
    bsz, seq, d = x.shape
    m = bsz * seq
    depth = w_in.shape[0]
    cos, sin = _rope_tables(seq)
    nchunks = seq // SSM_CHUNK
    tm_big = min(m, 1024)
    tm_mid = min(m, 512)

    x2d = x.reshape(m, d)
    for layer in range(depth):
        wt = w_in[layer].T.astype(BF16)
        tail = wt[COL_Q:]
        q_lo = 2 * SSM_HEADS
        k_lo = q_lo + ATTN_Q_HEADS * ATTN_HEAD_DIM
        v_lo = k_lo + ATTN_KV_HEADS * ATTN_HEAD_DIM
        wb_t = jnp.concatenate([
            _deinterleave_heads(tail[q_lo:k_lo], ATTN_Q_HEADS),
            _deinterleave_heads(tail[k_lo:v_lo], ATTN_KV_HEADS),
            tail[v_lo:],
            tail[:q_lo],
            jnp.zeros((DT_PAD - 2 * SSM_HEADS, d), BF16)], axis=0)
        later_weights = (w_ssm_up[layer], w_attn_up[layer], w_out[layer], w_mlp_in[layer], w_mlp_out[layer])
        proj, dt, wa_up, wb_up, wo, w1, w2 = _proj(x2d, g_mix[layer][None, :], wt, wb_t.T, later_weights,
                                                   tm=tm_big, tn=1024)
        proj3 = proj.reshape(bsz, seq, N_MAIN)

        dt5 = dt[:, :2 * SSM_HEADS].reshape(bsz, nchunks, SSM_CHUNK, 2, SSM_GROUPS, SSM_HPG)
        dt5 = dt5.transpose(0, 4, 1, 3, 5, 2).reshape(bsz, SSM_GROUPS, nchunks, 2 * SSM_HPG, SSM_CHUNK)
        y_ssm = _ssd(
            proj3, dt5, conv_w[layer], conv_b[layer][None, :],
            _group_rows(dt_bias_f[layer], dt_bias_b[layer]),
            _group_rows(a_log_f[layer], a_log_b[layer]),
            jnp.repeat(d_skip[layer].astype(F32), SSM_HEAD_DIM)[None, :],
            ssm_norm_g[layer][None, :])

        y_attn = _attn(proj3, cos, sin, _deinterleave_heads(q_norm_g[layer], 1)[None, :],
                       _deinterleave_heads(k_norm_g[layer], 1)[None, :],
                       tq=min(seq, 512), tk=min(seq, 512))

        merged = _merge(y_ssm.reshape(m, SSM_D_INNER), y_attn.reshape(m, ATTN_Q_HEADS * ATTN_HEAD_DIM),
                        wa_up, wb_up, proj, tm=tm_big, tn=512)
        x1, h2 = _outp(merged, wo, x2d, g_mlp[layer][None, :], tm=tm_mid)
        x2d = _mlp(h2, w1, w2, x1, g_final[None, :], tm=tm_big, tf=512, final_norm=layer == depth - 1)
    return x2d.reshape(bsz, seq, d)
```

```python
import functools

import jax
import jax.numpy as jnp
from jax import lax
from jax.experimental import pallas as pl
from jax.experimental.pallas import tpu as pltpu

F32 = jnp.float32
BF16 = jnp.bfloat16

EPS = 1e-6
LOG2E = 1.4426950408889634
D_MODEL = 2048
SSM_D_INNER = 2048
SSM_HEAD_DIM = 64
SSM_HEADS = SSM_D_INNER // SSM_HEAD_DIM
SSM_GROUPS = 8
SSM_HPG = SSM_HEADS // SSM_GROUPS
SSM_GW = SSM_D_INNER // SSM_GROUPS
SSM_STATE = 128
SSM_CONV_K = 5
SSM_CHUNK = 128
SSM_XBC = SSM_D_INNER + 2 * SSM_GROUPS * SSM_STATE
ATTN_HEAD_DIM = 128
ATTN_Q_HEADS = 16
ATTN_KV_HEADS = 4
ATTN_REP = ATTN_Q_HEADS // ATTN_KV_HEADS
ROPE_THETA = 10000.0
GRID_W = 64
D_FF = 4 * D_MODEL

COL_Z = 0
COL_XS = COL_Z + SSM_D_INNER
COL_B = COL_XS + SSM_D_INNER
COL_C = COL_B + SSM_GROUPS * SSM_STATE
COL_Q = COL_C + SSM_GROUPS * SSM_STATE
COL_K = COL_Q + ATTN_Q_HEADS * ATTN_HEAD_DIM
COL_V = COL_K + ATTN_KV_HEADS * ATTN_HEAD_DIM
COL_GA = COL_V + ATTN_KV_HEADS * ATTN_HEAD_DIM
COL_GB = COL_GA + D_MODEL
N_MAIN = COL_GB + D_MODEL
DT_PAD = 128

V7X_VMEM_LIMIT = 56 * 1024 * 1024
CONV_HALO = 16
CONV_WIN = SSM_CHUNK + 2 * CONV_HALO


def _params(*sem):
    return pltpu.CompilerParams(dimension_semantics=sem, vmem_limit_bytes=V7X_VMEM_LIMIT)


def _sigmoid(x):
    return 1.0 / (1.0 + jnp.exp(-x))


def _softplus(x):
    return jnp.maximum(x, 0.0) + jnp.log(1.0 + jnp.exp(-jnp.abs(x)))


_NT_DIMS = (((1,), (1,)), ((), ()))


def _proj_kernel(x_ref, g_ref, w_ref, wdt_ref, *rest, rows, ncast, nside):
    side_in, (o_ref, dt_ref), side_out, h_scr = rest[:nside], rest[nside:nside + 2], rest[nside + 2:-1], rest[-1]
    j = pl.program_id(1)

    @pl.when(pl.program_id(0) * pl.num_programs(1) + j < ncast)
    def _():
        for src, dst in zip(side_in, side_out):
            dst[...] = src[...].astype(dst.dtype)

    @pl.when(j == 0)
    def _():
        def body(r, _):
            sl = pl.ds(pl.multiple_of(r * rows, rows), rows)
            x = x_ref[sl, :]
            ms = jnp.mean(x * x, axis=-1, keepdims=True)
            hb = (x * lax.rsqrt(ms + EPS) * g_ref[...]).astype(BF16)
            h_scr[sl, :] = hb
            dt_ref[sl, :] = lax.dot_general(hb, wdt_ref[...], _NT_DIMS, preferred_element_type=F32)
            return 0

        lax.fori_loop(0, x_ref.shape[0] // rows, body, 0)

    o_ref[...] = lax.dot_general(h_scr[...], w_ref[...], _NT_DIMS,
                                 preferred_element_type=F32).astype(o_ref.dtype)


MAX_CAST_SLICES = 128


def _proj(x2d, g, w_t, side, *, tm, tn):
    m, d = x2d.shape
    n = N_MAIN
    nj = n // tn
    steps = (m // tm) * nj
    ncast = min(MAX_CAST_SLICES, 1 << (steps.bit_length() - 1))
    slice_map = lambda i, j: (jnp.minimum(i * nj + j, ncast - 1), 0)
    side_specs = [pl.BlockSpec((w.shape[0] // ncast, w.shape[1]), slice_map) for w in side]
    return pl.pallas_call(
        functools.partial(_proj_kernel, rows=min(tm, 256), ncast=ncast, nside=len(side)),
        grid=(m // tm, nj),
        in_specs=[
            pl.BlockSpec((tm, d), lambda i, j: (i, 0)),
            pl.BlockSpec((1, d), lambda i, j: (0, 0)),
            pl.BlockSpec((tn, d), lambda i, j: (j, 0)),
            pl.BlockSpec((DT_PAD, d), lambda i, j: (N_MAIN // DT_PAD, 0)),
        ] + side_specs,
        out_specs=[
            pl.BlockSpec((tm, tn), lambda i, j: (i, j)),
            pl.BlockSpec((tm, DT_PAD), lambda i, j: (i, 0)),
        ] + side_specs,
        out_shape=[
            jax.ShapeDtypeStruct((m, n), BF16),
            jax.ShapeDtypeStruct((m, DT_PAD), F32),
        ] + [jax.ShapeDtypeStruct(w.shape, BF16) for w in side],
        scratch_shapes=[pltpu.VMEM((tm, d), BF16)],
        compiler_params=_params("arbitrary", "arbitrary"),
        name="proj",
    )(x2d, g, w_t, w_t, *side)


_CONV_SIDE_TAPS = tuple(k for k in range(SSM_CONV_K) if k != SSM_CONV_K // 2)


def _conv_window(c, seq):
    L = SSM_CHUNK
    start = jnp.clip(c * L - CONV_HALO, 0, seq - CONV_WIN)
    start = pl.multiple_of(start, CONV_HALO)
    return start, (c * L - start) // CONV_HALO


def _shift_matrix(halo_units):
    L = SSM_CHUNK
    t_idx = lax.broadcasted_iota(jnp.int32, (L, CONV_WIN), 0)
    j_idx = lax.broadcasted_iota(jnp.int32, (L, CONV_WIN), 1)
    base = j_idx - t_idx - halo_units * CONV_HALO
    shifts = [jnp.where(base == k - SSM_CONV_K // 2, 1.0, 0.0).astype(BF16) for k in _CONV_SIDE_TAPS]
    return jnp.concatenate(shifts, axis=0)


def _conv_silu(shifted, centre, w_ref, b_ref):
    L = SSM_CHUNK
    mid = SSM_CONV_K // 2
    acc = centre * w_ref[mid:mid + 1, :] + b_ref[...]
    for n, k in enumerate(_CONV_SIDE_TAPS):
        acc = acc + shifted[n * L:(n + 1) * L] * w_ref[k:k + 1, :]
    return acc * _sigmoid(acc)


def _split3(a):
    a1 = a.astype(BF16)
    r1 = a - a1.astype(F32)
    a2 = r1.astype(BF16)
    a3 = (r1 - a2.astype(F32)).astype(BF16)
    return a1, a2, a3


def _ssd_kernel(z_ref, xs_ref, b_ref, c_ref, dt_ref, cwx_ref, cwb_ref, cwc_ref, cbx_ref, cbb_ref,
                cbc_ref, dtb_ref, alog_ref, dsk_ref, ng_ref, y_ref,
                shift_scr, x_scr, bt_scr, c_scr, rows_scr, cols_scr, yf_scr, yb_scr, hf_scr, hb_scr):
    L = SSM_CHUNK
    H = SSM_HPG
    N = SSM_STATE
    seq = xs_ref.shape[0]
    nchunks = seq // L

    row_i = lax.broadcasted_iota(jnp.int32, (L, L), 0)
    lane_i = lax.broadcasted_iota(jnp.int32, (L, L), 1)
    tri_pre = jnp.where(row_i <= lane_i, 1.0, 0.0).astype(BF16)
    tri_suf = jnp.where(row_i >= lane_i, 1.0, 0.0).astype(BF16)
    neg_a = -jnp.exp(alog_ref[...]) * LOG2E
    lane128 = lax.broadcasted_iota(jnp.int32, (L, L), 1)

    def col_bcast(cols, lane):
        return jnp.broadcast_to(cols[:, lane:lane + 1], (L, L))

    def expand(bc):
        lo = jnp.where(lane128 < SSM_HEAD_DIM, bc[0], bc[1])
        hi = jnp.where(lane128 < SSM_HEAD_DIM, bc[2], bc[3])
        return jnp.concatenate([lo, hi], axis=1)

    for halo_units in range(3):
        shift_scr[halo_units] = _shift_matrix(halo_units)

    def prep(c, _):
        rows_sl = pl.ds(pl.multiple_of(c * L, L), L)
        start, halo_units = _conv_window(c, seq)
        shift = shift_scr[halo_units]
        win = pl.ds(start, CONV_WIN)
        sx = jnp.dot(shift, xs_ref[win, :], preferred_element_type=F32)
        sbc = jnp.dot(shift, jnp.concatenate([b_ref[win, :], c_ref[win, :]], axis=1),
                      preferred_element_type=F32)
        xc = _conv_silu(sx, xs_ref[rows_sl, :].astype(F32), cwx_ref, cbx_ref)
        x_scr[rows_sl, :] = xc
        bm = _conv_silu(sbc[:, :N], b_ref[rows_sl, :].astype(F32), cwb_ref, cbb_ref)
        bt_scr[c] = bm.T.astype(BF16)
        c_scr[rows_sl, :] = _conv_silu(sbc[:, N:], c_ref[rows_sl, :].astype(F32), cwc_ref, cbc_ref).astype(BF16)

        dt = _softplus(dt_ref[c] + dtb_ref[...])
        a = neg_a * dt
        a3 = jnp.concatenate(_split3(a), axis=0)
        pre = jnp.dot(a3, tri_pre, preferred_element_type=F32)
        suf = jnp.dot(a3, tri_suf, preferred_element_type=F32)
        pre = pre[0:2 * H] + pre[2 * H:4 * H] + pre[4 * H:6 * H]
        suf = suf[0:2 * H] + suf[2 * H:4 * H] + suf[4 * H:6 * H]
        is_fwd = lax.broadcasted_iota(jnp.int32, (2 * H, L), 0) < H
        rows = jnp.concatenate([jnp.where(is_fwd, pre, suf), dt], axis=0)
        rows_scr[c] = rows
        padded = jnp.concatenate([rows, jnp.zeros((L - 4 * H, L), F32)], axis=0)
        cols_scr[rows_sl, :] = padded.T
        return 0

    lax.fori_loop(0, nchunks, prep, 0, unroll=8)

    row_h = lax.broadcasted_iota(jnp.int32, (H * L, SSM_GW), 0) // L
    lane_h = lax.broadcasted_iota(jnp.int32, (H * L, SSM_GW), 1) // SSM_HEAD_DIM

    def chunk(c, d, h_scr):
        rows_sl = pl.ds(pl.multiple_of(c * L, L), L)
        rows = rows_scr[c]
        cols = cols_scr[rows_sl, :]
        xc = x_scr[rows_sl, :]
        cm = c_scr[rows_sl, :]
        bt = bt_scr[c]
        scores = jnp.dot(cm, bt, preferred_element_type=F32)
        scores = jnp.where((lane_i <= row_i) if d == 0 else (lane_i >= row_i), scores, 0.0)
        cum_bc = [col_bcast(cols, d * H + h) for h in range(H)]
        dt_e = jnp.concatenate(
            [jnp.take_along_axis(cols, 2 * H + d * H + 2 * half + lane128 // SSM_HEAD_DIM, axis=1)
             for half in range(2)], axis=1)
        seg = (jnp.concatenate(cum_bc, axis=1)
               - jnp.concatenate([rows[d * H + h:d * H + h + 1, :] for h in range(H)], axis=1))
        decay = jnp.exp2(jnp.minimum(seg, 0.0))
        mix = jnp.concatenate([scores.astype(BF16)] * H, axis=1) * decay.astype(BF16)
        cum_e = expand(cum_bc)
        xdt = xc * dt_e
        xdt_b = xdt.astype(BF16)
        x_bd = jnp.where(row_h == lane_h, jnp.concatenate([xdt_b] * H, axis=0), jnp.zeros((), BF16))
        y = jnp.dot(mix, x_bd, preferred_element_type=F32)
        state = h_scr[...]
        y = y + jnp.dot(cm, state.astype(BF16), preferred_element_type=F32) * jnp.exp2(cum_e)
        tot = cum_e[L - 1:L, :] if d == 0 else cum_e[0:1, :]
        xe = (xdt * jnp.exp2(tot - cum_e)).astype(BF16)
        h_scr[...] = jnp.exp2(tot) * state + jnp.dot(bt, xe, preferred_element_type=F32)
        return y

    hf_scr[...] = jnp.zeros_like(hf_scr)
    hb_scr[...] = jnp.zeros_like(hb_scr)

    def scan(i, _):
        j = nchunks - 1 - i
        yf_scr[pl.ds(pl.multiple_of(i * L, L), L), :] = chunk(i, 0, hf_scr)
        yb_scr[pl.ds(pl.multiple_of(j * L, L), L), :] = chunk(j, 1, hb_scr)
        return 0

    lax.fori_loop(0, nchunks, scan, 0, unroll=8)

    def finish(c, _):
        rows_sl = pl.ds(pl.multiple_of(c * L, L), L)
        z = z_ref[rows_sl, :].astype(F32)
        y = yf_scr[rows_sl, :] + yb_scr[rows_sl, :] + dsk_ref[...] * x_scr[rows_sl, :]
        y = y * (z * _sigmoid(z))
        ms = jnp.mean(y * y, axis=-1, keepdims=True)
        y_ref[rows_sl, :] = (y * lax.rsqrt(ms + EPS) * ng_ref[...]).astype(y_ref.dtype)
        return 0

    lax.fori_loop(0, nchunks, finish, 0, unroll=4)


def _ssd(proj3, dt5, conv_w, conv_b, dtb, alog, dsk, ng):
    bsz, seq, _ = proj3.shape
    L, G, GW, N = SSM_CHUNK, SSM_GROUPS, SSM_GW, SSM_STATE
    nchunks = seq // L
    k = SSM_CONV_K
    col = lambda base, w: (lambda b, g: (b, 0, base // w + g))
    pcol = lambda base, w: (lambda b, g: (0, base // w + g))
    return pl.pallas_call(
        _ssd_kernel,
        grid=(bsz, G),
        in_specs=[
            pl.BlockSpec((None, seq, GW), col(COL_Z, GW)),
            pl.BlockSpec((None, seq, GW), col(COL_XS, GW)),
            pl.BlockSpec((None, seq, N), col(COL_B, N)),
            pl.BlockSpec((None, seq, N), col(COL_C, N)),
            pl.BlockSpec((None, None, nchunks, 2 * SSM_HPG, L), lambda b, g: (b, g, 0, 0, 0)),
            pl.BlockSpec((k, GW), pcol(0, GW)),
            pl.BlockSpec((k, N), pcol(SSM_D_INNER, N)),
            pl.BlockSpec((k, N), pcol(SSM_D_INNER + G * N, N)),
            pl.BlockSpec((1, GW), pcol(0, GW)),
            pl.BlockSpec((1, N), pcol(SSM_D_INNER, N)),
            pl.BlockSpec((1, N), pcol(SSM_D_INNER + G * N, N)),
            pl.BlockSpec((None, 2 * SSM_HPG, 1), lambda b, g: (g, 0, 0)),
            pl.BlockSpec((None, 2 * SSM_HPG, 1), lambda b, g: (g, 0, 0)),
            pl.BlockSpec((1, GW), pcol(0, GW)),
            pl.BlockSpec((1, GW), pcol(0, GW)),
        ],
        out_specs=pl.BlockSpec((None, seq, GW), lambda b, g: (b, 0, g)),
        out_shape=jax.ShapeDtypeStruct((bsz, seq, SSM_D_INNER), BF16),
        scratch_shapes=[
            pltpu.VMEM((3, len(_CONV_SIDE_TAPS) * L, CONV_WIN), BF16),
            pltpu.VMEM((seq, GW), F32),
            pltpu.VMEM((nchunks, N, L), BF16),
            pltpu.VMEM((seq, N), BF16),
            pltpu.VMEM((nchunks, 4 * SSM_HPG, L), F32),
            pltpu.VMEM((seq, L), F32),
            pltpu.VMEM((seq, GW), F32),
            pltpu.VMEM((seq, GW), F32),
            pltpu.VMEM((N, GW), F32),
            pltpu.VMEM((N, GW), F32),
        ],
        compiler_params=_params("parallel", "parallel"),
        name="ssd",
    )(proj3, proj3, proj3, proj3, dt5, conv_w, conv_w, conv_w, conv_b, conv_b, conv_b, dtb, alog, dsk, ng)


def _rms_rope(x, g, cos, sin):
    ms = jnp.mean(x * x, axis=-1, keepdims=True)
    x = x * lax.rsqrt(ms + EPS) * g
    return x * cos + pltpu.roll(x, ATTN_HEAD_DIM // 2, axis=1) * sin


def _attn_kernel(qc_ref, qn_ref, k_ref, v_ref, cos_ref, sin_ref, gq_ref, gk_ref, o_ref,
                       q_scr, k_scr, vx_scr, s_scr, *, tq, tk, rows):
    seq = k_ref.shape[0]
    hd = ATTN_HEAD_DIM
    nrow = ATTN_REP * tq
    qi = pl.program_id(2)
    nq = pl.num_programs(2)
    scale = hd ** -0.5 * LOG2E

    def prep_q(q_ref, tile, slot):
        sl = pl.ds(pl.multiple_of(tile * tq, tq), tq)
        cos = cos_ref[sl, :]
        sin = sin_ref[sl, :]
        for r in range(ATTN_REP):
            q = _rms_rope(q_ref[:, r * hd:(r + 1) * hd].astype(F32), gq_ref[...], cos, sin)
            q_scr[slot, r * tq:(r + 1) * tq, :] = (q * scale).astype(BF16)

    @pl.when(qi == 0)
    def _():
        def body(r, _):
            sl = pl.ds(pl.multiple_of(r * rows, rows), rows)
            k = _rms_rope(k_ref[sl, :].astype(F32), gk_ref[...], cos_ref[sl, :], sin_ref[sl, :])
            k_scr[sl, :] = k.astype(BF16)
            return 0

        lax.fori_loop(0, seq // rows, body, 0)
        vx_scr[:, :hd] = v_ref[...]
        vx_scr[:, hd:] = jnp.ones((seq, hd), BF16)
        prep_q(qc_ref, 0, 0)

    q_all = q_scr[qi % 2]

    m_acc = jnp.full((nrow, hd), -jnp.inf, F32)
    for kb in range(seq // tk):
        s = lax.dot_general(q_all, k_scr[kb * tk:(kb + 1) * tk, :], (((1,), (1,)), ((), ())),
                            preferred_element_type=F32)
        s_scr[:, kb * tk:(kb + 1) * tk] = s
        for lt in range(tk // hd):
            m_acc = jnp.maximum(m_acc, s[:, lt * hd:(lt + 1) * hd])
    m = jnp.max(m_acc, axis=-1, keepdims=True)

    prep_q(qn_ref, jnp.minimum(qi + 1, nq - 1), (qi + 1) % 2)

    half = nrow // 2
    acc_lo = jnp.zeros((half, 2 * hd), F32)
    acc_hi = jnp.zeros((half, 2 * hd), F32)
    for kb in range(seq // tk):
        pb = jnp.exp2(s_scr[:, kb * tk:(kb + 1) * tk] - m).astype(BF16)
        vb = vx_scr[kb * tk:(kb + 1) * tk, :]
        acc_lo = acc_lo + jnp.dot(pb[:half], vb, preferred_element_type=F32)
        acc_hi = acc_hi + jnp.dot(pb[half:], vb, preferred_element_type=F32)
    acc = jnp.concatenate([acc_lo, acc_hi], axis=0)
    out = acc[:, :hd] / acc[:, hd:]
    for r in range(ATTN_REP):
        o_ref[:, r * hd:(r + 1) * hd] = out[r * tq:(r + 1) * tq].astype(o_ref.dtype)


def _attn(proj3, cos, sin, gq, gk, *, tq, tk):
    bsz, seq, _ = proj3.shape
    hd = ATTN_HEAD_DIM
    qw = ATTN_REP * hd
    nq = seq // tq
    return pl.pallas_call(
        functools.partial(_attn_kernel, tq=tq, tk=tk, rows=min(seq, 512)),
        grid=(bsz, ATTN_KV_HEADS, nq),
        in_specs=[
            pl.BlockSpec((None, tq, qw), lambda b, h, i: (b, i, COL_Q // qw + h)),
            pl.BlockSpec((None, tq, qw), lambda b, h, i: (b, jnp.minimum(i + 1, nq - 1), COL_Q // qw + h)),
            pl.BlockSpec((None, seq, hd), lambda b, h, i: (b, 0, COL_K // hd + h)),
            pl.BlockSpec((None, seq, hd), lambda b, h, i: (b, 0, COL_V // hd + h)),
            pl.BlockSpec((seq, hd), lambda b, h, i: (0, 0)),
            pl.BlockSpec((seq, hd), lambda b, h, i: (0, 0)),
            pl.BlockSpec((1, hd), lambda b, h, i: (0, 0)),
            pl.BlockSpec((1, hd), lambda b, h, i: (0, 0)),
        ],
        out_specs=pl.BlockSpec((None, tq, qw), lambda b, h, i: (b, i, h)),
        out_shape=jax.ShapeDtypeStruct((bsz, seq, ATTN_Q_HEADS * hd), BF16),
        scratch_shapes=[
            pltpu.VMEM((2, ATTN_REP * tq, hd), BF16),
            pltpu.VMEM((seq, hd), BF16),
            pltpu.VMEM((seq, 2 * hd), BF16),
            pltpu.VMEM((ATTN_REP * tq, seq), F32),
        ],
        compiler_params=_params("parallel", "parallel", "arbitrary"),
        name="attn",
    )(proj3, proj3, proj3, proj3, cos, sin, gq, gk)


def _merge_kernel(ys_ref, ya_ref, wa_ref, wb_ref, ga_ref, gb_ref, o_ref):
    a = jnp.dot(ys_ref[...], wa_ref[...], preferred_element_type=F32)
    b = jnp.dot(ya_ref[...], wb_ref[...], preferred_element_type=F32)
    merged = _sigmoid(ga_ref[...].astype(F32)) * a + _sigmoid(gb_ref[...].astype(F32)) * b
    o_ref[...] = merged.astype(o_ref.dtype)


def _merge(ys, ya, wa, wb, proj, *, tm, tn):
    m, d = ys.shape
    return pl.pallas_call(
        _merge_kernel,
        grid=(m // tm, D_MODEL // tn),
        in_specs=[
            pl.BlockSpec((tm, d), lambda i, j: (i, 0)),
            pl.BlockSpec((tm, d), lambda i, j: (i, 0)),
            pl.BlockSpec((d, tn), lambda i, j: (0, j)),
            pl.BlockSpec((d, tn), lambda i, j: (0, j)),
            pl.BlockSpec((tm, tn), lambda i, j: (i, COL_GA // tn + j)),
            pl.BlockSpec((tm, tn), lambda i, j: (i, COL_GB // tn + j)),
        ],
        out_specs=pl.BlockSpec((tm, tn), lambda i, j: (i, j)),
        out_shape=jax.ShapeDtypeStruct((m, D_MODEL), BF16),
        compiler_params=_params("parallel", "arbitrary"),
        name="merge",
    )(ys, ya, wa, wb, proj, proj)


def _outp_kernel(mg_ref, wo_ref, x_ref, g_ref, x1_ref, h2_ref):
    x1 = x_ref[...] + jnp.dot(mg_ref[...], wo_ref[...], preferred_element_type=F32)
    x1_ref[...] = x1
    ms = jnp.mean(x1 * x1, axis=-1, keepdims=True)
    h2_ref[...] = (x1 * lax.rsqrt(ms + EPS) * g_ref[...]).astype(h2_ref.dtype)


def _outp(merged, wo, x2d, g, *, tm):
    m, d = x2d.shape
    return pl.pallas_call(
        _outp_kernel,
        grid=(m // tm,),
        in_specs=[
            pl.BlockSpec((tm, d), lambda i: (i, 0)),
            pl.BlockSpec((d, d), lambda i: (0, 0)),
            pl.BlockSpec((tm, d), lambda i: (i, 0)),
            pl.BlockSpec((1, d), lambda i: (0, 0)),
        ],
        out_specs=[
            pl.BlockSpec((tm, d), lambda i: (i, 0)),
            pl.BlockSpec((tm, d), lambda i: (i, 0)),
        ],
        out_shape=[
            jax.ShapeDtypeStruct((m, d), F32),
            jax.ShapeDtypeStruct((m, d), BF16),
        ],
        compiler_params=_params("parallel"),
        name="outp",
    )(merged, wo, x2d, g)


def _mlp_kernel(h2_ref, w1_ref, w2_ref, x1_ref, g_ref, o_ref, *, final_norm):
    f = pl.program_id(1)

    @pl.when(f == 0)
    def _():
        o_ref[...] = x1_ref[...]

    act = jnp.maximum(jnp.dot(h2_ref[...], w1_ref[...], preferred_element_type=F32), 0.0)
    o_ref[...] += jnp.dot((act * act).astype(BF16), w2_ref[...], preferred_element_type=F32)

    if final_norm:
        @pl.when(f == pl.num_programs(1) - 1)
        def _():
            y = o_ref[...]
            ms = jnp.mean(y * y, axis=-1, keepdims=True)
            o_ref[...] = y * lax.rsqrt(ms + EPS) * g_ref[...]


def _mlp(h2, w1, w2, x1, g, *, tm, tf, final_norm):
    m, d = x1.shape
    ff = w1.shape[1]
    return pl.pallas_call(
        functools.partial(_mlp_kernel, final_norm=final_norm),
        grid=(m // tm, ff // tf),
        in_specs=[
            pl.BlockSpec((tm, d), lambda i, f: (i, 0)),
            pl.BlockSpec((d, tf), lambda i, f: (0, f)),
            pl.BlockSpec((tf, d), lambda i, f: (f, 0)),
            pl.BlockSpec((tm, d), lambda i, f: (i, 0)),
            pl.BlockSpec((1, d), lambda i, f: (0, 0)),
        ],
        out_specs=pl.BlockSpec((tm, d), lambda i, f: (i, 0)),
        out_shape=jax.ShapeDtypeStruct((m, d), F32),
        compiler_params=_params("parallel", "arbitrary"),
        name="mlp",
    )(h2, w1, w2, x1, g)


def _rope_tables(seq):
    rows = seq // GRID_W
    row = jnp.repeat(jnp.arange(rows, dtype=F32), GRID_W)
    col = jnp.tile(jnp.arange(GRID_W, dtype=F32), rows)
    half = ATTN_HEAD_DIM // 2
    inv_freq = 1.0 / (ROPE_THETA ** (jnp.arange(0, half, 2, dtype=F32) / half))
    ang = jnp.concatenate([row[:, None] * inv_freq, col[:, None] * inv_freq], axis=-1)
    cos = jnp.concatenate([jnp.cos(ang), jnp.cos(ang)], axis=-1)
    sin = jnp.concatenate([-jnp.sin(ang), jnp.sin(ang)], axis=-1)
    return cos, sin


def _deinterleave_heads(a, nheads):
    tail = a.shape[1:]
    a = a.reshape((nheads, ATTN_HEAD_DIM // 2, 2) + tail)
    return jnp.swapaxes(a, 1, 2).reshape((nheads * ATTN_HEAD_DIM,) + tail)


def _group_rows(v_f, v_b):
    g = jnp.concatenate([v_f.reshape(SSM_GROUPS, SSM_HPG), v_b.reshape(SSM_GROUPS, SSM_HPG)], axis=1)
    return g[:, :, None].astype(F32)


def kernel(x, g_mix, w_in, conv_w, conv_b, dt_bias_f, dt_bias_b, a_log_f, a_log_b, d_skip, ssm_norm_g,
           q_norm_g, k_norm_g, w_ssm_up, w_attn_up, w_out, g_mlp, w_mlp_in, w_mlp_out, g_final):
    bsz, seq, d = x.shape
    m = bsz * seq
    depth = w_in.shape[0]
    cos, sin = _rope_tables(seq)
    nchunks = seq // SSM_CHUNK
    tm_big = min(m, 1024)
    tm_mid = min(m, 512)

    x2d = x.reshape(m, d)
    for layer in range(depth):
        wt = w_in[layer].T.astype(BF16)
        tail = wt[COL_Q:]
        q_lo = 2 * SSM_HEADS
        k_lo = q_lo + ATTN_Q_HEADS * ATTN_HEAD_DIM
        v_lo = k_lo + ATTN_KV_HEADS * ATTN_HEAD_DIM
        w_t = jnp.concatenate([
            wt[:COL_Q],
            _deinterleave_heads(tail[q_lo:k_lo], ATTN_Q_HEADS),
            _deinterleave_heads(tail[k_lo:v_lo], ATTN_KV_HEADS),
            tail[v_lo:],
            tail[:q_lo],
            jnp.zeros((DT_PAD - 2 * SSM_HEADS, d), BF16)], axis=0)
        later_weights = (w_ssm_up[layer], w_attn_up[layer], w_out[layer], w_mlp_in[layer], w_mlp_out[layer])
        proj, dt, wa_up, wb_up, wo, w1, w2 = _proj(x2d, g_mix[layer][None, :], w_t, later_weights,
                                                   tm=tm_big, tn=1024)
        proj3 = proj.reshape(bsz, seq, N_MAIN)

        dt5 = dt[:, :2 * SSM_HEADS].reshape(bsz, nchunks, SSM_CHUNK, 2, SSM_GROUPS, SSM_HPG)
        dt5 = dt5.transpose(0, 4, 1, 3, 5, 2).reshape(bsz, SSM_GROUPS, nchunks, 2 * SSM_HPG, SSM_CHUNK)
        y_ssm = _ssd(
            proj3, dt5, conv_w[layer], conv_b[layer][None, :],
            _group_rows(dt_bias_f[layer], dt_bias_b[layer]),
            _group_rows(a_log_f[layer], a_log_b[layer]),
            jnp.repeat(d_skip[layer].astype(F32), SSM_HEAD_DIM)[None, :],
            ssm_norm_g[layer][None, :])

        y_attn = _attn(proj3, cos, sin, _deinterleave_heads(q_norm_g[layer], 1)[None, :],
                       _deinterleave_heads(k_norm_g[layer], 1)[None, :],
                       tq=min(seq, 512), tk=min(seq, 512))

        merged = _merge(y_ssm.reshape(m, SSM_D_INNER), y_attn.reshape(m, ATTN_Q_HEADS * ATTN_HEAD_DIM),
                        wa_up, wb_up, proj, tm=tm_big, tn=512)
        x1, h2 = _outp(merged, wo, x2d, g_mlp[layer][None, :], tm=tm_mid)
        x2d = _mlp(h2, w1, w2, x1, g_final[None, :], tm=tm_big, tf=512, final_norm=layer == depth - 1)
    return x2d.reshape(bsz, seq, d)
```

```python
import functools

import jax
import jax.numpy as jnp
from jax import lax
from jax.experimental import pallas as pl
from jax.experimental.pallas import tpu as pltpu

F32 = jnp.float32
BF16 = jnp.bfloat16

EPS = 1e-6
LOG2E = 1.4426950408889634
D_MODEL = 2048
SSM_D_INNER = 2048
SSM_HEAD_DIM = 64
SSM_HEADS = SSM_D_INNER // SSM_HEAD_DIM
SSM_GROUPS = 8
SSM_HPG = SSM_HEADS // SSM_GROUPS
SSM_GW = SSM_D_INNER // SSM_GROUPS
SSM_STATE = 128
SSM_CONV_K = 5
SSM_CHUNK = 128
SSM_XBC = SSM_D_INNER + 2 * SSM_GROUPS * SSM_STATE
ATTN_HEAD_DIM = 128
ATTN_Q_HEADS = 16
ATTN_KV_HEADS = 4
ATTN_REP = ATTN_Q_HEADS // ATTN_KV_HEADS
ROPE_THETA = 10000.0
GRID_W = 64
D_FF = 4 * D_MODEL

COL_Z = 0
COL_XS = COL_Z + SSM_D_INNER
COL_B = COL_XS + SSM_D_INNER
COL_C = COL_B + SSM_GROUPS * SSM_STATE
COL_Q = COL_C + SSM_GROUPS * SSM_STATE
COL_K = COL_Q + ATTN_Q_HEADS * ATTN_HEAD_DIM
COL_V = COL_K + ATTN_KV_HEADS * ATTN_HEAD_DIM
COL_GA = COL_V + ATTN_KV_HEADS * ATTN_HEAD_DIM
COL_GB = COL_GA + D_MODEL
N_MAIN = COL_GB + D_MODEL
DT_PAD = 128

V7X_VMEM_LIMIT = 56 * 1024 * 1024
CONV_HALO = 16
CONV_WIN = SSM_CHUNK + 2 * CONV_HALO


def _params(*sem):
    return pltpu.CompilerParams(dimension_semantics=sem, vmem_limit_bytes=V7X_VMEM_LIMIT)


def _sigmoid(x):
    return 1.0 / (1.0 + jnp.exp(-x))


def _softplus(x):
    return jnp.maximum(x, 0.0) + jnp.log(1.0 + jnp.exp(-jnp.abs(x)))


_NT_DIMS = (((1,), (1,)), ((), ()))


def _proj_kernel(x_ref, g_ref, w_ref, wdt_ref, *rest, rows, ncast, nside):
    side_in, (o_ref, dt_ref), side_out, h_scr = rest[:nside], rest[nside:nside + 2], rest[nside + 2:-1], rest[-1]
    j = pl.program_id(1)

    @pl.when(pl.program_id(0) * pl.num_programs(1) + j < ncast)
    def _():
        for src, dst in zip(side_in, side_out):
            dst[...] = src[...].astype(dst.dtype)

    @pl.when(j == 0)
    def _():
        def body(r, _):
            sl = pl.ds(pl.multiple_of(r * rows, rows), rows)
            x = x_ref[sl, :]
            ms = jnp.mean(x * x, axis=-1, keepdims=True)
            hb = (x * lax.rsqrt(ms + EPS) * g_ref[...]).astype(BF16)
            h_scr[sl, :] = hb
            dt_ref[sl, :] = lax.dot_general(hb, wdt_ref[...], _NT_DIMS, preferred_element_type=F32)
            return 0

        lax.fori_loop(0, x_ref.shape[0] // rows, body, 0)

    o_ref[...] = lax.dot_general(h_scr[...], w_ref[...], _NT_DIMS,
                                 preferred_element_type=F32).astype(o_ref.dtype)


MAX_CAST_SLICES = 128


def _proj(x2d, g, w_t, side, *, tm, tn):
    m, d = x2d.shape
    n = N_MAIN
    nj = n // tn
    steps = (m // tm) * nj
    ncast = min(MAX_CAST_SLICES, 1 << (steps.bit_length() - 1))
    slice_map = lambda i, j: (jnp.minimum(i * nj + j, ncast - 1), 0)
    side_specs = [pl.BlockSpec((w.shape[0] // ncast, w.shape[1]), slice_map) for w in side]
    return pl.pallas_call(
        functools.partial(_proj_kernel, rows=min(tm, 256), ncast=ncast, nside=len(side)),
        grid=(m // tm, nj),
        in_specs=[
            pl.BlockSpec((tm, d), lambda i, j: (i, 0)),
            pl.BlockSpec((1, d), lambda i, j: (0, 0)),
            pl.BlockSpec((tn, d), lambda i, j: (j, 0)),
            pl.BlockSpec((DT_PAD, d), lambda i, j: (N_MAIN // DT_PAD, 0)),
        ] + side_specs,
        out_specs=[
            pl.BlockSpec((tm, tn), lambda i, j: (i, j)),
            pl.BlockSpec((tm, DT_PAD), lambda i, j: (i, 0)),
        ] + side_specs,
        out_shape=[
            jax.ShapeDtypeStruct((m, n), BF16),
            jax.ShapeDtypeStruct((m, DT_PAD), F32),
        ] + [jax.ShapeDtypeStruct(w.shape, BF16) for w in side],
        scratch_shapes=[pltpu.VMEM((tm, d), BF16)],
        compiler_params=_params("arbitrary", "arbitrary"),
        name="proj",
    )(x2d, g, w_t, w_t, *side)


_CONV_SIDE_TAPS = tuple(k for k in range(SSM_CONV_K) if k != SSM_CONV_K // 2)


def _conv_window(c, seq):
    L = SSM_CHUNK
    start = jnp.clip(c * L - CONV_HALO, 0, seq - CONV_WIN)
    start = pl.multiple_of(start, CONV_HALO)
    return start, (c * L - start) // CONV_HALO


def _shift_matrix(halo_units):
    L = SSM_CHUNK
    t_idx = lax.broadcasted_iota(jnp.int32, (L, CONV_WIN), 0)
    j_idx = lax.broadcasted_iota(jnp.int32, (L, CONV_WIN), 1)
    base = j_idx - t_idx - halo_units * CONV_HALO
    shifts = [jnp.where(base == k - SSM_CONV_K // 2, 1.0, 0.0).astype(BF16) for k in _CONV_SIDE_TAPS]
    return jnp.concatenate(shifts, axis=0)


def _conv_silu(shifted, centre, w_ref, b_ref):
    L = SSM_CHUNK
    mid = SSM_CONV_K // 2
    acc = centre * w_ref[mid:mid + 1, :] + b_ref[...]
    for n, k in enumerate(_CONV_SIDE_TAPS):
        acc = acc + shifted[n * L:(n + 1) * L] * w_ref[k:k + 1, :]
    return acc * _sigmoid(acc)


def _split3(a):
    a1 = a.astype(BF16)
    r1 = a - a1.astype(F32)
    a2 = r1.astype(BF16)
    a3 = (r1 - a2.astype(F32)).astype(BF16)
    return a1, a2, a3


def _ssd_kernel(z_ref, xs_ref, b_ref, c_ref, dt_ref, cwx_ref, cwb_ref, cwc_ref, cbx_ref, cbb_ref,
                cbc_ref, dtb_ref, alog_ref, dsk_ref, ng_ref, y_ref,
                shift_scr, x_scr, bt_scr, c_scr, rows_scr, cols_scr, yf_scr, yb_scr, hf_scr, hb_scr):
    L = SSM_CHUNK
    H = SSM_HPG
    N = SSM_STATE
    seq = xs_ref.shape[0]
    nchunks = seq // L

    row_i = lax.broadcasted_iota(jnp.int32, (L, L), 0)
    lane_i = lax.broadcasted_iota(jnp.int32, (L, L), 1)
    tri_pre = jnp.where(row_i <= lane_i, 1.0, 0.0).astype(BF16)
    tri_suf = jnp.where(row_i >= lane_i, 1.0, 0.0).astype(BF16)
    neg_a = -jnp.exp(alog_ref[...]) * LOG2E
    lane128 = lax.broadcasted_iota(jnp.int32, (L, L), 1)

    def col_bcast(cols, lane):
        return jnp.broadcast_to(cols[:, lane:lane + 1], (L, L))

    def expand(bc):
        lo = jnp.where(lane128 < SSM_HEAD_DIM, bc[0], bc[1])
        hi = jnp.where(lane128 < SSM_HEAD_DIM, bc[2], bc[3])
        return jnp.concatenate([lo, hi], axis=1)

    for halo_units in range(3):
        shift_scr[halo_units] = _shift_matrix(halo_units)

    def prep(c, _):
        rows_sl = pl.ds(pl.multiple_of(c * L, L), L)
        start, halo_units = _conv_window(c, seq)
        shift = shift_scr[halo_units]
        win = pl.ds(start, CONV_WIN)
        sx = jnp.dot(shift, xs_ref[win, :], preferred_element_type=F32)
        sbc = jnp.dot(shift, jnp.concatenate([b_ref[win, :], c_ref[win, :]], axis=1),
                      preferred_element_type=F32)
        xc = _conv_silu(sx, xs_ref[rows_sl, :].astype(F32), cwx_ref, cbx_ref)
        x_scr[rows_sl, :] = xc
        bm = _conv_silu(sbc[:, :N], b_ref[rows_sl, :].astype(F32), cwb_ref, cbb_ref)
        bt_scr[c] = bm.T.astype(BF16)
        c_scr[rows_sl, :] = _conv_silu(sbc[:, N:], c_ref[rows_sl, :].astype(F32), cwc_ref, cbc_ref).astype(BF16)

        dt = _softplus(dt_ref[c] + dtb_ref[...])
        a = neg_a * dt
        a3 = jnp.concatenate(_split3(a), axis=0)
        pre = jnp.dot(a3, tri_pre, preferred_element_type=F32)
        suf = jnp.dot(a3, tri_suf, preferred_element_type=F32)
        pre = pre[0:2 * H] + pre[2 * H:4 * H] + pre[4 * H:6 * H]
        suf = suf[0:2 * H] + suf[2 * H:4 * H] + suf[4 * H:6 * H]
        is_fwd = lax.broadcasted_iota(jnp.int32, (2 * H, L), 0) < H
        rows = jnp.concatenate([jnp.where(is_fwd, pre, suf), dt], axis=0)
        rows_scr[c] = rows
        padded = jnp.concatenate([rows, jnp.zeros((L - 4 * H, L), F32)], axis=0)
        cols_scr[rows_sl, :] = padded.T
        return 0

    lax.fori_loop(0, nchunks, prep, 0, unroll=8)

    row_h = lax.broadcasted_iota(jnp.int32, (H * L, SSM_GW), 0) // L
    lane_h = lax.broadcasted_iota(jnp.int32, (H * L, SSM_GW), 1) // SSM_HEAD_DIM

    def chunk(c, d, h_scr):
        rows_sl = pl.ds(pl.multiple_of(c * L, L), L)
        rows = rows_scr[c]
        cols = cols_scr[rows_sl, :]
        xc = x_scr[rows_sl, :]
        cm = c_scr[rows_sl, :]
        bt = bt_scr[c]
        scores = jnp.dot(cm, bt, preferred_element_type=F32)
        scores = jnp.where((lane_i <= row_i) if d == 0 else (lane_i >= row_i), scores, 0.0)
        cum_bc = [col_bcast(cols, d * H + h) for h in range(H)]
        dt_e = jnp.concatenate(
            [jnp.take_along_axis(cols, 2 * H + d * H + 2 * half + lane128 // SSM_HEAD_DIM, axis=1)
             for half in range(2)], axis=1)
        seg = (jnp.concatenate(cum_bc, axis=1)
               - jnp.concatenate([rows[d * H + h:d * H + h + 1, :] for h in range(H)], axis=1))
        decay = jnp.exp2(jnp.minimum(seg, 0.0))
        mix = jnp.concatenate([scores.astype(BF16)] * H, axis=1) * decay.astype(BF16)
        cum_e = expand(cum_bc)
        xdt = xc * dt_e
        xdt_b = xdt.astype(BF16)
        x_bd = jnp.where(row_h == lane_h, jnp.concatenate([xdt_b] * H, axis=0), jnp.zeros((), BF16))
        y = jnp.dot(mix, x_bd, preferred_element_type=F32)
        state = h_scr[...]
        y = y + jnp.dot(cm, state.astype(BF16), preferred_element_type=F32) * jnp.exp2(cum_e)
        tot = cum_e[L - 1:L, :] if d == 0 else cum_e[0:1, :]
        xe = (xdt * jnp.exp2(tot - cum_e)).astype(BF16)
        h_scr[...] = jnp.exp2(tot) * state + jnp.dot(bt, xe, preferred_element_type=F32)
        return y

    hf_scr[...] = jnp.zeros_like(hf_scr)
    hb_scr[...] = jnp.zeros_like(hb_scr)

    def scan(i, _):
        j = nchunks - 1 - i
        yf_scr[pl.ds(pl.multiple_of(i * L, L), L), :] = chunk(i, 0, hf_scr)
        yb_scr[pl.ds(pl.multiple_of(j * L, L), L), :] = chunk(j, 1, hb_scr)
        return 0

    lax.fori_loop(0, nchunks, scan, 0, unroll=8)

    def finish(c, _):
        rows_sl = pl.ds(pl.multiple_of(c * L, L), L)
        z = z_ref[rows_sl, :].astype(F32)
        y = yf_scr[rows_sl, :] + yb_scr[rows_sl, :] + dsk_ref[...] * x_scr[rows_sl, :]
        y = y * (z * _sigmoid(z))
        ms = jnp.mean(y * y, axis=-1, keepdims=True)
        y_ref[rows_sl, :] = (y * lax.rsqrt(ms + EPS) * ng_ref[...]).astype(y_ref.dtype)
        return 0

    lax.fori_loop(0, nchunks, finish, 0, unroll=4)


def _ssd(proj3, dt5, conv_w, conv_b, dtb, alog, dsk, ng):
    bsz, seq, _ = proj3.shape
    L, G, GW, N = SSM_CHUNK, SSM_GROUPS, SSM_GW, SSM_STATE
    nchunks = seq // L
    k = SSM_CONV_K
    col = lambda base, w: (lambda b, g: (b, 0, base // w + g))
    pcol = lambda base, w: (lambda b, g: (0, base // w + g))
    return pl.pallas_call(
        _ssd_kernel,
        grid=(bsz, G),
        in_specs=[
            pl.BlockSpec((None, seq, GW), col(COL_Z, GW)),
            pl.BlockSpec((None, seq, GW), col(COL_XS, GW)),
            pl.BlockSpec((None, seq, N), col(COL_B, N)),
            pl.BlockSpec((None, seq, N), col(COL_C, N)),
            pl.BlockSpec((None, None, nchunks, 2 * SSM_HPG, L), lambda b, g: (b, g, 0, 0, 0)),
            pl.BlockSpec((k, GW), pcol(0, GW)),
            pl.BlockSpec((k, N), pcol(SSM_D_INNER, N)),
            pl.BlockSpec((k, N), pcol(SSM_D_INNER + G * N, N)),
            pl.BlockSpec((1, GW), pcol(0, GW)),
            pl.BlockSpec((1, N), pcol(SSM_D_INNER, N)),
            pl.BlockSpec((1, N), pcol(SSM_D_INNER + G * N, N)),
            pl.BlockSpec((None, 2 * SSM_HPG, 1), lambda b, g: (g, 0, 0)),
            pl.BlockSpec((None, 2 * SSM_HPG, 1), lambda b, g: (g, 0, 0)),
            pl.BlockSpec((1, GW), pcol(0, GW)),
            pl.BlockSpec((1, GW), pcol(0, GW)),
        ],
        out_specs=pl.BlockSpec((None, seq, GW), lambda b, g: (b, 0, g)),
        out_shape=jax.ShapeDtypeStruct((bsz, seq, SSM_D_INNER), BF16),
        scratch_shapes=[
            pltpu.VMEM((3, len(_CONV_SIDE_TAPS) * L, CONV_WIN), BF16),
            pltpu.VMEM((seq, GW), F32),
            pltpu.VMEM((nchunks, N, L), BF16),
            pltpu.VMEM((seq, N), BF16),
            pltpu.VMEM((nchunks, 4 * SSM_HPG, L), F32),
            pltpu.VMEM((seq, L), F32),
            pltpu.VMEM((seq, GW), F32),
            pltpu.VMEM((seq, GW), F32),
            pltpu.VMEM((N, GW), F32),
            pltpu.VMEM((N, GW), F32),
        ],
        compiler_params=_params("parallel", "parallel"),
        name="ssd",
    )(proj3, proj3, proj3, proj3, dt5, conv_w, conv_w, conv_w, conv_b, conv_b, conv_b, dtb, alog, dsk, ng)


def _rms_rope(x, g, cos, sin):
    half = ATTN_HEAD_DIM // 2
    ms = jnp.mean(x * x, axis=-1, keepdims=True)
    x = x * lax.rsqrt(ms + EPS) * g
    lane = lax.broadcasted_iota(jnp.int32, x.shape, 1)
    x = jnp.take_along_axis(x, jnp.where(lane < half, 2 * lane, 2 * (lane - half) + 1), axis=1)
    return x * cos + pltpu.roll(x, half, axis=1) * sin


def _attn_kernel(qc_ref, qn_ref, k_ref, v_ref, cos_ref, sin_ref, gq_ref, gk_ref, o_ref,
                       q_scr, k_scr, vx_scr, s_scr, *, tq, tk, rows):
    seq = k_ref.shape[0]
    hd = ATTN_HEAD_DIM
    nrow = ATTN_REP * tq
    qi = pl.program_id(2)
    nq = pl.num_programs(2)
    scale = hd ** -0.5 * LOG2E

    def prep_q(q_ref, tile, slot):
        sl = pl.ds(pl.multiple_of(tile * tq, tq), tq)
        cos = cos_ref[sl, :]
        sin = sin_ref[sl, :]
        for r in range(ATTN_REP):
            q = _rms_rope(q_ref[:, r * hd:(r + 1) * hd].astype(F32), gq_ref[...], cos, sin)
            q_scr[slot, r * tq:(r + 1) * tq, :] = (q * scale).astype(BF16)

    @pl.when(qi == 0)
    def _():
        def body(r, _):
            sl = pl.ds(pl.multiple_of(r * rows, rows), rows)
            k = _rms_rope(k_ref[sl, :].astype(F32), gk_ref[...], cos_ref[sl, :], sin_ref[sl, :])
            k_scr[sl, :] = k.astype(BF16)
            return 0

        lax.fori_loop(0, seq // rows, body, 0)
        vx_scr[:, :hd] = v_ref[...]
        vx_scr[:, hd:] = jnp.ones((seq, hd), BF16)
        prep_q(qc_ref, 0, 0)

    q_all = q_scr[qi % 2]

    m_acc = jnp.full((nrow, hd), -jnp.inf, F32)
    for kb in range(seq // tk):
        s = lax.dot_general(q_all, k_scr[kb * tk:(kb + 1) * tk, :], (((1,), (1,)), ((), ())),
                            preferred_element_type=F32)
        s_scr[:, kb * tk:(kb + 1) * tk] = s
        for lt in range(tk // hd):
            m_acc = jnp.maximum(m_acc, s[:, lt * hd:(lt + 1) * hd])
    m = jnp.max(m_acc, axis=-1, keepdims=True)

    prep_q(qn_ref, jnp.minimum(qi + 1, nq - 1), (qi + 1) % 2)

    half = nrow // 2
    acc_lo = jnp.zeros((half, 2 * hd), F32)
    acc_hi = jnp.zeros((half, 2 * hd), F32)
    for kb in range(seq // tk):
        pb = jnp.exp2(s_scr[:, kb * tk:(kb + 1) * tk] - m).astype(BF16)
        vb = vx_scr[kb * tk:(kb + 1) * tk, :]
        acc_lo = acc_lo + jnp.dot(pb[:half], vb, preferred_element_type=F32)
        acc_hi = acc_hi + jnp.dot(pb[half:], vb, preferred_element_type=F32)
    acc = jnp.concatenate([acc_lo, acc_hi], axis=0)
    out = acc[:, :hd] / acc[:, hd:]
    for r in range(ATTN_REP):
        o_ref[:, r * hd:(r + 1) * hd] = out[r * tq:(r + 1) * tq].astype(o_ref.dtype)


def _attn(proj3, cos, sin, gq, gk, *, tq, tk):
    bsz, seq, _ = proj3.shape
    hd = ATTN_HEAD_DIM
    qw = ATTN_REP * hd
    nq = seq // tq
    return pl.pallas_call(
        functools.partial(_attn_kernel, tq=tq, tk=tk, rows=min(seq, 512)),
        grid=(bsz, ATTN_KV_HEADS, nq),
        in_specs=[
            pl.BlockSpec((None, tq, qw), lambda b, h, i: (b, i, COL_Q // qw + h)),
            pl.BlockSpec((None, tq, qw), lambda b, h, i: (b, jnp.minimum(i + 1, nq - 1), COL_Q // qw + h)),
            pl.BlockSpec((None, seq, hd), lambda b, h, i: (b, 0, COL_K // hd + h)),
            pl.BlockSpec((None, seq, hd), lambda b, h, i: (b, 0, COL_V // hd + h)),
            pl.BlockSpec((seq, hd), lambda b, h, i: (0, 0)),
            pl.BlockSpec((seq, hd), lambda b, h, i: (0, 0)),
            pl.BlockSpec((1, hd), lambda b, h, i: (0, 0)),
            pl.BlockSpec((1, hd), lambda b, h, i: (0, 0)),
        ],
        out_specs=pl.BlockSpec((None, tq, qw), lambda b, h, i: (b, i, h)),
        out_shape=jax.ShapeDtypeStruct((bsz, seq, ATTN_Q_HEADS * hd), BF16),
        scratch_shapes=[
            pltpu.VMEM((2, ATTN_REP * tq, hd), BF16),
            pltpu.VMEM((seq, hd), BF16),
            pltpu.VMEM((seq, 2 * hd), BF16),
            pltpu.VMEM((ATTN_REP * tq, seq), F32),
        ],
        compiler_params=_params("parallel", "parallel", "arbitrary"),
        name="attn",
    )(proj3, proj3, proj3, proj3, cos, sin, gq, gk)


def _merge_kernel(ys_ref, ya_ref, wa_ref, wb_ref, ga_ref, gb_ref, o_ref):
    a = jnp.dot(ys_ref[...], wa_ref[...], preferred_element_type=F32)
    b = jnp.dot(ya_ref[...], wb_ref[...], preferred_element_type=F32)
    merged = _sigmoid(ga_ref[...].astype(F32)) * a + _sigmoid(gb_ref[...].astype(F32)) * b
    o_ref[...] = merged.astype(o_ref.dtype)


def _merge(ys, ya, wa, wb, proj, *, tm, tn):
    m, d = ys.shape
    return pl.pallas_call(
        _merge_kernel,
        grid=(m // tm, D_MODEL // tn),
        in_specs=[
            pl.BlockSpec((tm, d), lambda i, j: (i, 0)),
            pl.BlockSpec((tm, d), lambda i, j: (i, 0)),
            pl.BlockSpec((d, tn), lambda i, j: (0, j)),
            pl.BlockSpec((d, tn), lambda i, j: (0, j)),
            pl.BlockSpec((tm, tn), lambda i, j: (i, COL_GA // tn + j)),
            pl.BlockSpec((tm, tn), lambda i, j: (i, COL_GB // tn + j)),
        ],
        out_specs=pl.BlockSpec((tm, tn), lambda i, j: (i, j)),
        out_shape=jax.ShapeDtypeStruct((m, D_MODEL), BF16),
        compiler_params=_params("parallel", "arbitrary"),
        name="merge",
    )(ys, ya, wa, wb, proj, proj)


def _outp_kernel(mg_ref, wo_ref, x_ref, g_ref, x1_ref, h2_ref):
    x1 = x_ref[...] + jnp.dot(mg_ref[...], wo_ref[...], preferred_element_type=F32)
    x1_ref[...] = x1
    ms = jnp.mean(x1 * x1, axis=-1, keepdims=True)
    h2_ref[...] = (x1 * lax.rsqrt(ms + EPS) * g_ref[...]).astype(h2_ref.dtype)


def _outp(merged, wo, x2d, g, *, tm):
    m, d = x2d.shape
    return pl.pallas_call(
        _outp_kernel,
        grid=(m // tm,),
        in_specs=[
            pl.BlockSpec((tm, d), lambda i: (i, 0)),
            pl.BlockSpec((d, d), lambda i: (0, 0)),
            pl.BlockSpec((tm, d), lambda i: (i, 0)),
            pl.BlockSpec((1, d), lambda i: (0, 0)),
        ],
        out_specs=[
            pl.BlockSpec((tm, d), lambda i: (i, 0)),
            pl.BlockSpec((tm, d), lambda i: (i, 0)),
        ],
        out_shape=[
            jax.ShapeDtypeStruct((m, d), F32),
            jax.ShapeDtypeStruct((m, d), BF16),
        ],
        compiler_params=_params("parallel"),
        name="outp",
    )(merged, wo, x2d, g)


def _mlp_kernel(h2_ref, w1_ref, w2_ref, x1_ref, g_ref, o_ref, *, final_norm):
    f = pl.program_id(1)

    @pl.when(f == 0)
    def _():
        o_ref[...] = x1_ref[...]

    act = jnp.maximum(jnp.dot(h2_ref[...], w1_ref[...], preferred_element_type=F32), 0.0)
    o_ref[...] += jnp.dot((act * act).astype(BF16), w2_ref[...], preferred_element_type=F32)

    if final_norm:
        @pl.when(f == pl.num_programs(1) - 1)
        def _():
            y = o_ref[...]
            ms = jnp.mean(y * y, axis=-1, keepdims=True)
            o_ref[...] = y * lax.rsqrt(ms + EPS) * g_ref[...]


def _mlp(h2, w1, w2, x1, g, *, tm, tf, final_norm):
    m, d = x1.shape
    ff = w1.shape[1]
    return pl.pallas_call(
        functools.partial(_mlp_kernel, final_norm=final_norm),
        grid=(m // tm, ff // tf),
        in_specs=[
            pl.BlockSpec((tm, d), lambda i, f: (i, 0)),
            pl.BlockSpec((d, tf), lambda i, f: (0, f)),
            pl.BlockSpec((tf, d), lambda i, f: (f, 0)),
            pl.BlockSpec((tm, d), lambda i, f: (i, 0)),
            pl.BlockSpec((1, d), lambda i, f: (0, 0)),
        ],
        out_specs=pl.BlockSpec((tm, d), lambda i, f: (i, 0)),
        out_shape=jax.ShapeDtypeStruct((m, d), F32),
        compiler_params=_params("parallel", "arbitrary"),
        name="mlp",
    )(h2, w1, w2, x1, g)


def _rope_tables(seq):
    rows = seq // GRID_W
    row = jnp.repeat(jnp.arange(rows, dtype=F32), GRID_W)
    col = jnp.tile(jnp.arange(GRID_W, dtype=F32), rows)
    half = ATTN_HEAD_DIM // 2
    inv_freq = 1.0 / (ROPE_THETA ** (jnp.arange(0, half, 2, dtype=F32) / half))
    ang = jnp.concatenate([row[:, None] * inv_freq, col[:, None] * inv_freq], axis=-1)
    cos = jnp.concatenate([jnp.cos(ang), jnp.cos(ang)], axis=-1)
    sin = jnp.concatenate([-jnp.sin(ang), jnp.sin(ang)], axis=-1)
    return cos, sin


def _group_rows(v_f, v_b):
    g = jnp.concatenate([v_f.reshape(SSM_GROUPS, SSM_HPG), v_b.reshape(SSM_GROUPS, SSM_HPG)], axis=1)
    return g[:, :, None].astype(F32)


def kernel(x, g_mix, w_in, conv_w, conv_b, dt_bias_f, dt_bias_b, a_log_f, a_log_b, d_skip, ssm_norm_g,
           q_norm_g, k_norm_g, w_ssm_up, w_attn_up, w_out, g_mlp, w_mlp_in, w_mlp_out, g_final):
    bsz, seq, d = x.shape
    m = bsz * seq
    depth = w_in.shape[0]
    cos, sin = _rope_tables(seq)
    nchunks = seq // SSM_CHUNK
    tm_big = min(m, 1024)
    tm_mid = min(m, 512)

    x2d = x.reshape(m, d)
    for layer in range(depth):
        wt = w_in[layer].T.astype(BF16)
        tail = wt[COL_Q:]
        q_lo = 2 * SSM_HEADS
        w_t = jnp.concatenate([wt[:COL_Q], tail[q_lo:], tail[:q_lo],
                               jnp.zeros((DT_PAD - 2 * SSM_HEADS, d), BF16)], axis=0)
        later_weights = (w_ssm_up[layer], w_attn_up[layer], w_out[layer], w_mlp_in[layer], w_mlp_out[layer])
        proj, dt, wa_up, wb_up, wo, w1, w2 = _proj(x2d, g_mix[layer][None, :], w_t, later_weights,
                                                   tm=tm_big, tn=1024)
        proj3 = proj.reshape(bsz, seq, N_MAIN)

        dt5 = dt[:, :2 * SSM_HEADS].reshape(bsz, nchunks, SSM_CHUNK, 2, SSM_GROUPS, SSM_HPG)
        dt5 = dt5.transpose(0, 4, 1, 3, 5, 2).reshape(bsz, SSM_GROUPS, nchunks, 2 * SSM_HPG, SSM_CHUNK)
        y_ssm = _ssd(
            proj3, dt5, conv_w[layer], conv_b[layer][None, :],
            _group_rows(dt_bias_f[layer], dt_bias_b[layer]),
            _group_rows(a_log_f[layer], a_log_b[layer]),
            jnp.repeat(d_skip[layer].astype(F32), SSM_HEAD_DIM)[None, :],
            ssm_norm_g[layer][None, :])

        y_attn = _attn(proj3, cos, sin, q_norm_g[layer][None, :], k_norm_g[layer][None, :],
                       tq=min(seq, 512), tk=min(seq, 512))

        merged = _merge(y_ssm.reshape(m, SSM_D_INNER), y_attn.reshape(m, ATTN_Q_HEADS * ATTN_HEAD_DIM),
                        wa_up, wb_up, proj, tm=tm_big, tn=512)
        x1, h2 = _outp(merged, wo, x2d, g_mlp[layer][None, :], tm=tm_mid)
        x2d = _mlp(h2, w1, w2, x1, g_final[None, :], tm=tm_big, tf=512, final_norm=layer == depth - 1)
    return x2d.reshape(bsz, seq, d)
```
